```python
import math, functools
import jax, jax.numpy as jnp
from jax import lax
import numpy as np

D_MODEL = 1024
BATCH = 8
SEQ = 8192
DEPTH = 1
DEC_BATCH = 128
DEC_SEQ = 8
PAST_LEN = 8192
PAGE_SIZE = 128

SB_HD = 64
SB_WIDTH = D_MODEL // 2
SB_HEADS = SB_WIDTH // SB_HD
SB_QBLOCK = 128
SB_BIAS_INIT = -7.0
RET_HEADS = 4
RET_WIDTH = D_MODEL - SB_WIDTH
RET_DK = RET_WIDTH // RET_HEADS
RET_DV = RET_DK
RET_CHUNK = 128
ROT_BASE = 10000.0
MIX_WIDTH = SB_WIDTH + RET_WIDTH
PROJ_WIDTH = 3 * SB_WIDTH + 4 * RET_WIDTH
D_FF = ((8 * D_MODEL // 3 + 127) // 128) * 128
CONV_W = 3
ALPHA = (2.0 * DEPTH) ** 0.25
BETA = (8.0 * DEPTH) ** -0.25
LN_EPS = 1e-5

kernel_name = 'stickbreak_retention_convffn_hybrid_step'


def layer_norm(x, g, b):
    xf = x.astype(jnp.float32)
    mu = jnp.mean(xf, -1, keepdims=True)
    var = jnp.mean(jnp.square(xf - mu), -1, keepdims=True)
    out = (xf - mu) * lax.rsqrt(var + LN_EPS) * g.astype(jnp.float32) + b.astype(jnp.float32)
    return out.astype(x.dtype)


def head_norm(o):
    mu = jnp.mean(o, -1, keepdims=True)
    var = jnp.mean(jnp.square(o - mu), -1, keepdims=True)
    return (o - mu) * lax.rsqrt(var + LN_EPS)


def rotate(x, pos):
    half = x.shape[-1] // 2
    inv = 1.0 / (ROT_BASE ** jnp.linspace(0.0, 1.0, half, dtype=jnp.float32))
    ang = pos.astype(jnp.float32)[:, None] * inv[None, :]
    cos, sin = jnp.cos(ang)[None, :, None, :], jnp.sin(ang)[None, :, None, :]
    xf = x.astype(jnp.float32)
    x1, x2 = xf[..., :half], xf[..., half:]
    return jnp.concatenate([x1 * cos - x2 * sin, x2 * cos + x1 * sin], -1)


def retention_chunkwise(q, k, v, s0):
    B, T, H, DK = q.shape
    C = RET_CHUNK if T % RET_CHUNK == 0 else T
    n = T // C
    lg = jnp.log(1.0 - 2.0 ** (-5.0 - jnp.arange(H, dtype=jnp.float32)))
    idx = jnp.arange(C, dtype=jnp.float32)
    diff = idx[:, None] - idx[None, :]
    decay_mask = jnp.where(diff[None] >= 0,
                           jnp.exp(jnp.maximum(diff, 0.0)[None] * lg[:, None, None]), 0.0)
    q_decay = jnp.exp((idx[:, None] + 1.0) * lg[None, :])
    k_decay = jnp.exp((C - 1.0 - idx)[:, None] * lg[None, :])
    c_decay = jnp.exp(C * lg)

    def to_chunks(a):
        return a.reshape(B, n, C, H, a.shape[-1]).swapaxes(0, 1)

    def step(S, xs):
        qc, kc, vc = xs
        att = jnp.einsum('bihd,bjhd->bhij', qc, kc) * decay_mask
        intra = jnp.einsum('bhij,bjhv->bihv', att, vc)
        cross = jnp.einsum('bihd,bhdv->bihv', qc, S) * q_decay[None, :, :, None]
        S = S * c_decay[None, :, None, None] + jnp.einsum(
            'bjhd,bjhv->bhdv', kc * k_decay[None, :, :, None], vc)
        return S, intra + cross

    S, o = lax.scan(step, s0, (to_chunks(q), to_chunks(k), to_chunks(v)))
    return o.swapaxes(0, 1).reshape(B, T, H, v.shape[-1]), S


def stick_breaking_block(qb, k, v, bias, qpos, kpos):
    z = jnp.einsum('bqhd,bkhd->bhqk', qb, k) * (SB_HD ** -0.5) + bias[None, :, None, None]
    mask = kpos[None, :] < qpos[:, None]
    log_stay = jnp.where(mask, jax.nn.log_sigmoid(-z), 0.0)
    suffix = lax.cumsum(log_stay, axis=3, reverse=True) - log_stay
    w = jnp.where(mask, jnp.exp(jax.nn.log_sigmoid(z) + suffix), 0.0)
    return jnp.einsum('bhqk,bkhd->bqhd', w, v)


def sb_attention(q, k, v, bias, q_offset):
    B, Tq, H, D = q.shape
    QB = SB_QBLOCK if Tq % SB_QBLOCK == 0 else Tq
    nb = Tq // QB
    kf, vf = k.astype(jnp.float32), v.astype(jnp.float32)
    bf = bias.astype(jnp.float32)
    kpos = jnp.arange(k.shape[1])
    qblocks = q.astype(jnp.float32).reshape(B, nb, QB, H, D).swapaxes(0, 1)

    def one(args):
        qb, bi = args
        qpos = q_offset + bi * QB + jnp.arange(QB)
        return stick_breaking_block(qb, kf, vf, bf, qpos, kpos)

    o = lax.map(one, (qblocks, jnp.arange(nb)))
    return o.swapaxes(0, 1).reshape(B, Tq, H, D).astype(q.dtype)


def prompt_sb_attention(q, k, v, bias):
    return sb_attention(q, k, v, bias, 0)


def paged_sb_attention(q, k, v, bias, cache_k, cache_v, page_table):
    def per_seq(args):
        pt, qs, ks, vs = args
        kp = cache_k[pt].reshape(-1, SB_HEADS, SB_HD).astype(ks.dtype)
        vp = cache_v[pt].reshape(-1, SB_HEADS, SB_HD).astype(vs.dtype)
        k_all = jnp.concatenate([kp, ks], 0)[None]
        v_all = jnp.concatenate([vp, vs], 0)[None]
        return sb_attention(qs[None], k_all, v_all, bias, kp.shape[0])[0]

    return lax.map(per_seq, (page_table, q, k, v))


def hybrid_layer(x, pos0, attend, s_ret0, conv_buf, w_in, sb_bias, w_o, ln1_g, ln1_b,
                 w_up, conv_w, conv_b, w_down, ln2_g, ln2_b):
    B, T, _ = x.shape
    f32 = jnp.float32
    proj = x @ w_in
    splits = [SB_WIDTH, 2 * SB_WIDTH, 3 * SB_WIDTH,
              3 * SB_WIDTH + RET_WIDTH, 3 * SB_WIDTH + 2 * RET_WIDTH, 3 * SB_WIDTH + 3 * RET_WIDTH]
    q_sb, k_sb, v_sb, q_r, k_r, v_r, g_r = jnp.split(proj, splits, axis=-1)
    q_sb = q_sb.reshape(B, T, SB_HEADS, SB_HD)
    k_sb = k_sb.reshape(B, T, SB_HEADS, SB_HD)
    v_sb = v_sb.reshape(B, T, SB_HEADS, SB_HD)
    o_sb = attend(q_sb, k_sb, v_sb, sb_bias)

    pos = pos0 + jnp.arange(T)
    qr = rotate(q_r.reshape(B, T, RET_HEADS, RET_DK), pos)
    kr = rotate(k_r.reshape(B, T, RET_HEADS, RET_DK), pos) * (RET_DK ** -0.5)
    vr = v_r.reshape(B, T, RET_HEADS, RET_DV).astype(f32)
    o_r, s_new = retention_chunkwise(qr, kr, vr, s_ret0.astype(f32))
    o_r = head_norm(o_r) * jax.nn.silu(g_r.astype(f32)).reshape(B, T, RET_HEADS, RET_DV)

    mix = jnp.concatenate([o_sb.reshape(B, T, SB_WIDTH),
                           o_r.reshape(B, T, RET_WIDTH).astype(x.dtype)], -1) @ w_o
    h = layer_norm(ALPHA * x + mix, ln1_g, ln1_b)

    u = h @ w_up
    u_ext = jnp.concatenate([conv_buf.astype(u.dtype), u], 1)
    c = conv_b + sum(conv_w[i] * u_ext[:, i:i + T] for i in range(CONV_W))
    f = (jax.nn.gelu(c[..., :D_FF]) * c[..., D_FF:]) @ w_down
    y = layer_norm(ALPHA * h + f, ln2_g, ln2_b)
    return y, k_sb, v_sb, s_new, u_ext[:, T:]


def setup_inputs(seed: int = 0) -> dict:
    key = jax.random.key(seed)
    ks = jax.random.split(key, 20)
    f32 = jnp.float32
    n_pages = PAST_LEN // PAGE_SIZE
    n_phys = (DEC_BATCH * n_pages * 5) // 4
    x_prompt = jax.random.normal(ks[0], (BATCH, SEQ, D_MODEL), f32)
    x_sample = jax.random.normal(ks[1], (DEC_BATCH, DEC_SEQ, D_MODEL), f32)
    cache_k = jax.random.normal(ks[2], (DEPTH, n_phys, PAGE_SIZE, SB_HEADS, SB_HD), f32)
    cache_v = BETA * jax.random.normal(ks[3], (DEPTH, n_phys, PAGE_SIZE, SB_HEADS, SB_HD), f32)
    perm = jax.random.permutation(ks[4], n_phys)[: DEC_BATCH * n_pages]
    page_table = perm.reshape(DEC_BATCH, n_pages).astype(jnp.int32)
    state_ret = 0.5 * jax.random.normal(ks[5], (DEPTH, DEC_BATCH, RET_HEADS, RET_DK, RET_DV), f32)
    state_conv = BETA * jax.random.normal(ks[6], (DEPTH, DEC_BATCH, CONV_W - 1, 2 * D_FF), f32)
    col_scale = jnp.concatenate([
        jnp.ones((2 * SB_WIDTH,), f32), jnp.full((SB_WIDTH,), BETA, f32),
        jnp.ones((2 * RET_WIDTH,), f32), jnp.full((RET_WIDTH,), BETA, f32),
        jnp.ones((RET_WIDTH,), f32)])
    w_in = jax.random.normal(ks[7], (DEPTH, D_MODEL, PROJ_WIDTH), f32) * (D_MODEL ** -0.5) * col_scale
    sb_bias = SB_BIAS_INIT + 0.1 * jax.random.normal(ks[17], (DEPTH, SB_HEADS), f32)
    w_o = jax.random.normal(ks[8], (DEPTH, MIX_WIDTH, D_MODEL), f32) * (MIX_WIDTH ** -0.5 * BETA)
    ln1_g = 1.0 + 0.02 * jax.random.normal(ks[9], (DEPTH, D_MODEL), f32)
    ln1_b = 0.02 * jax.random.normal(ks[10], (DEPTH, D_MODEL), f32)
    w_up = jax.random.normal(ks[11], (DEPTH, D_MODEL, 2 * D_FF), f32) * (D_MODEL ** -0.5 * BETA)
    conv_w = jax.random.normal(ks[12], (DEPTH, CONV_W, 2 * D_FF), f32) * (CONV_W ** -0.5)
    conv_b = 0.01 * jax.random.normal(ks[13], (DEPTH, 2 * D_FF), f32)
    w_down = jax.random.normal(ks[14], (DEPTH, D_FF, D_MODEL), f32) * (D_FF ** -0.5 * BETA)
    ln2_g = 1.0 + 0.02 * jax.random.normal(ks[15], (DEPTH, D_MODEL), f32)
    ln2_b = 0.02 * jax.random.normal(ks[16], (DEPTH, D_MODEL), f32)
    return {'x_prompt': x_prompt, 'x_sample': x_sample, 'cache_k': cache_k, 'cache_v': cache_v,
            'page_table': page_table, 'state_ret': state_ret, 'state_conv': state_conv,
            'w_in': w_in, 'sb_bias': sb_bias, 'w_o': w_o, 'ln1_g': ln1_g, 'ln1_b': ln1_b,
            'w_up': w_up, 'conv_w': conv_w, 'conv_b': conv_b, 'w_down': w_down,
            'ln2_g': ln2_g, 'ln2_b': ln2_b}


def reference(x_prompt, x_sample, cache_k, cache_v, page_table, state_ret, state_conv,
              w_in, sb_bias, w_o, ln1_g, ln1_b, w_up, conv_w, conv_b, w_down, ln2_g, ln2_b):
    xp, xs = x_prompt, x_sample
    kp_l, vp_l, sp_l, cp_l, ks_l, vs_l, ss_l, cs_l = [], [], [], [], [], [], [], []
    for l in range(DEPTH):
        lw = (w_in[l], sb_bias[l], w_o[l], ln1_g[l], ln1_b[l], w_up[l], conv_w[l], conv_b[l],
              w_down[l], ln2_g[l], ln2_b[l])
        s0 = jnp.zeros((xp.shape[0], RET_HEADS, RET_DK, RET_DV), jnp.float32)
        c0 = jnp.zeros((xp.shape[0], CONV_W - 1, 2 * D_FF), xp.dtype)
        xp, kp, vp, sp, cp = hybrid_layer(xp, 0, prompt_sb_attention, s0, c0, *lw)
        attend = functools.partial(paged_sb_attention, cache_k=cache_k[l], cache_v=cache_v[l],
                                   page_table=page_table)
        xs, kk, vv, ss, cs = hybrid_layer(xs, PAST_LEN, attend, state_ret[l], state_conv[l], *lw)
        kp_l.append(kp); vp_l.append(vp); sp_l.append(sp.astype(state_ret.dtype)); cp_l.append(cp)
        ks_l.append(kk); vs_l.append(vv); ss_l.append(ss.astype(state_ret.dtype)); cs_l.append(cs)
    return (xp, xs, jnp.stack(kp_l), jnp.stack(vp_l), jnp.stack(sp_l), jnp.stack(cp_l),
            jnp.stack(ks_l), jnp.stack(vs_l), jnp.stack(ss_l), jnp.stack(cs_l))
```

```python
import functools

import jax
import jax.numpy as jnp
from jax import lax
from jax.experimental import pallas as pl
from jax.experimental.pallas import tpu as pltpu

F32 = jnp.float32
BF16 = jnp.bfloat16

D_MODEL = 1024
DEPTH = 1
SB_HD = 64
SB_WIDTH = D_MODEL // 2
SB_HEADS = SB_WIDTH // SB_HD
SB_SCALE = SB_HD ** -0.5
RET_HEADS = 4
RET_WIDTH = D_MODEL - SB_WIDTH
RET_DK = RET_WIDTH // RET_HEADS
RET_CHUNK = 128
ROT_BASE = 10000.0
PROJ_WIDTH = 3 * SB_WIDTH + 4 * RET_WIDTH
D_FF = ((8 * D_MODEL // 3 + 127) // 128) * 128
CONV_W = 3
ALPHA = (2.0 * DEPTH) ** 0.25
LN_EPS = 1e-5

LANES = 128
SUBLANES = 8
VMEM_LIMIT = 56 * 1024 * 1024

ROW_TILE = 512
FFN_ROW_TILE = 256
FFN_CHUNKS = 2
SB_TILE = 256
DEC_PAGES_PER_STEP = 8


def _params(*sem):
    return pltpu.CompilerParams(dimension_semantics=sem, vmem_limit_bytes=VMEM_LIMIT)


def _resident(shape):
    nd = len(shape)
    return pl.BlockSpec(shape, lambda *_: (0,) * nd, pipeline_mode=pl.Buffered(1))


def _layer_norm(x, g, b):
    mu = jnp.mean(x, axis=-1, keepdims=True)
    d = x - mu
    var = jnp.mean(d * d, axis=-1, keepdims=True)
    return d * lax.rsqrt(var + LN_EPS) * g + b


def _proj_kernel(x_ref, w_ref, cos_ref, sin_ref,
                 q_o, k_o, v_o, kb_o, vb_o, qr_o, kr_o, vr_o, g_o):
    xb = x_ref[...].astype(BF16)

    def mm(c0, width):
        return jnp.dot(xb, w_ref[:, c0:c0 + width], preferred_element_type=F32)

    q_o[...] = (mm(0, SB_WIDTH) * SB_SCALE).astype(BF16)
    k = mm(SB_WIDTH, SB_WIDTH)
    k_o[...] = k
    kb_o[...] = k.astype(BF16)
    v = mm(2 * SB_WIDTH, SB_WIDTH)
    v_o[...] = v
    vb_o[...] = v.astype(BF16)

    cos = cos_ref[...]
    sin = sin_ref[...]

    def rotate_into(out_ref, xr, scale):
        for h in range(RET_HEADS):
            xh = xr[:, h * RET_DK:(h + 1) * RET_DK]
            r = xh * cos + pltpu.roll(xh, RET_DK // 2, axis=1) * sin
            if scale is not None:
                r = r * scale
            out_ref[:, h * RET_DK:(h + 1) * RET_DK] = r.astype(BF16)

    base = 3 * SB_WIDTH
    rotate_into(qr_o, mm(base, RET_WIDTH), None)
    rotate_into(kr_o, mm(base + RET_WIDTH, RET_WIDTH), RET_DK ** -0.5)
    vr_o[...] = mm(base + 2 * RET_WIDTH, RET_WIDTH).astype(BF16)
    g_o[...] = mm(base + 3 * RET_WIDTH, RET_WIDTH)


def _proj(x2, w_in_bf, cos_t, sin_t):
    n = x2.shape[0]
    tm = min(ROW_TILE, n)
    n_tab = cos_t.shape[0] // tm
    row = lambda i: (i, 0)
    tab = lambda i: (i % n_tab, 0)
    half = lambda dt: jax.ShapeDtypeStruct((n, SB_WIDTH), dt)
    spec = pl.BlockSpec((tm, SB_WIDTH), row)
    return pl.pallas_call(
        _proj_kernel,
        grid=(n // tm,),
        in_specs=[pl.BlockSpec((tm, D_MODEL), row), _resident(w_in_bf.shape),
                  pl.BlockSpec((tm, RET_DK), tab), pl.BlockSpec((tm, RET_DK), tab)],
        out_specs=[spec] * 9,
        out_shape=[half(BF16), half(F32), half(F32), half(BF16), half(BF16),
                   half(BF16), half(BF16), half(BF16), half(F32)],
        compiler_params=_params("parallel"),
        name="proj",
    )(x2, w_in_bf, cos_t, sin_t)


def _sb_block(qh, k, v, bias, u2, carry, mask):
    s = lax.dot_general(qh, k, (((1,), (1,)), ((), ())), preferred_element_type=F32)
    z = s + bias
    e = jnp.exp(-jnp.abs(z))
    ls = jnp.minimum(z, 0.0) - jnp.log(1.0 + e)
    lstay = ls - z
    if mask is not None:
        lstay = jnp.where(mask, lstay, 0.0)
    hi = lstay.astype(BF16)
    lo = (lstay - hi.astype(F32)).astype(BF16)
    suf = jnp.dot(jnp.concatenate([hi, lo], axis=1), u2, preferred_element_type=F32)
    w = jnp.exp(ls + suf + carry)
    if mask is not None:
        w = jnp.where(mask, w, 0.0)
    pv = jnp.dot(w.astype(BF16), v, preferred_element_type=F32)
    tot = jnp.sum(lstay, axis=1, keepdims=True)
    return pv, tot


def _suffix_matrix(n):
    j = lax.broadcasted_iota(jnp.int32, (n, n), 0)
    s = lax.broadcasted_iota(jnp.int32, (n, n), 1)
    u = (j > s).astype(BF16)
    return jnp.concatenate([u, u], axis=0)


def _sb_prompt_kernel(bias_ref, q_ref, k_ref, v_ref, u2_ref, o_ref, acc_ref, carry_ref):
    pair = pl.program_id(1)
    qi = pl.program_id(2)
    tq = q_ref.shape[0]
    q = q_ref[...]
    lane = lax.broadcasted_iota(jnp.int32, q.shape, 1)
    zero = jnp.zeros_like(q)
    qs = (jnp.where(lane < SB_HD, q, zero), jnp.where(lane >= SB_HD, q, zero))
    biases = (bias_ref[2 * pair], bias_ref[2 * pair + 1])
    u2 = u2_ref[...]
    row = lax.broadcasted_iota(jnp.int32, (tq, tq), 0)
    col = lax.broadcasted_iota(jnp.int32, (tq, tq), 1)
    causal = col < row
    acc_ref[...] = jnp.zeros_like(acc_ref)
    carry_ref[...] = jnp.zeros_like(carry_ref)

    def block(kb, mask):
        start = pl.multiple_of(kb * tq, tq)
        k = k_ref[pl.ds(start, tq), :]
        v = v_ref[pl.ds(start, tq), :]
        for h in range(2):
            c = carry_ref[h]
            pv, tot = _sb_block(qs[h], k, v, biases[h], u2,
                                jnp.concatenate([c] * (tq // LANES), axis=1), mask)
            acc_ref[h] += pv
            carry_ref[h] = c + tot

    block(qi, causal)

    def body(j, carry):
        block(qi - j, None)
        return carry

    lax.fori_loop(1, qi + 1, body, 0)
    o_ref[...] = jnp.where(lane < SB_HD, acc_ref[0], acc_ref[1]).astype(BF16)


def _sb_prompt(q3, k3, v3, bias):
    b, t, _ = q3.shape
    tq = SB_TILE
    n_pairs = SB_WIDTH // LANES
    u2 = _suffix_matrix(tq)
    qspec = pl.BlockSpec((None, tq, LANES), lambda bi, p, qi: (bi, qi, p))
    kvspec = pl.BlockSpec((None, t, LANES), lambda bi, p, qi: (bi, 0, p))
    return pl.pallas_call(
        _sb_prompt_kernel,
        grid=(b, n_pairs, t // tq),
        in_specs=[pl.BlockSpec(memory_space=pltpu.SMEM), qspec, kvspec, kvspec, _resident(u2.shape)],
        out_specs=qspec,
        out_shape=jax.ShapeDtypeStruct(q3.shape, BF16),
        scratch_shapes=[pltpu.VMEM((2, tq, LANES), F32), pltpu.VMEM((2, tq, LANES), F32)],
        compiler_params=_params("parallel", "parallel", "arbitrary"),
        name="sb_prompt",
    )(bias, q3, k3, v3, u2)


def _sb_decode_kernel(pt_ref, bias_ref, q_ref, kn_ref, vn_ref, u2_ref, *rest, n_pages_step):
    del pt_ref
    g = n_pages_step
    k_refs, v_refs = rest[:g], rest[g:2 * g]
    o_ref, qbd_ref, biasv_ref, acc_ref, carry_ref = rest[2 * g:]
    step = pl.program_id(1)
    t_new = q_ref.shape[0]
    rows = SB_HEADS * t_new

    def page(kb, vb, mask):
        pv, tot = _sb_block(qbd_ref[...], kb, vb, biasv_ref[...], u2_ref[...], carry_ref[...], mask)
        acc_ref[...] += pv
        carry_ref[...] += tot

    @pl.when(step == 0)
    def _init():
        q = q_ref[...].astype(F32)
        q_rep = jnp.concatenate([q] * SB_HEADS, axis=0)
        r = lax.broadcasted_iota(jnp.int32, (rows, SB_WIDTH), 0) // t_new
        c = lax.broadcasted_iota(jnp.int32, (rows, SB_WIDTH), 1) // SB_HD
        qbd_ref[...] = jnp.where(r == c, q_rep, 0.0).astype(BF16)
        rh = lax.broadcasted_iota(jnp.int32, (rows, LANES), 0) // t_new
        bv = jnp.zeros((rows, LANES), F32)
        for h in range(SB_HEADS):
            bv = jnp.where(rh == h, bias_ref[h], bv)
        biasv_ref[...] = bv
        acc_ref[...] = jnp.zeros_like(acc_ref)
        carry_ref[...] = jnp.zeros_like(carry_ref)
        pad = jnp.zeros((LANES - t_new, SB_WIDTH), F32)
        kn = jnp.concatenate([kn_ref[...].astype(F32), pad], axis=0).astype(BF16)
        vn = jnp.concatenate([vn_ref[...].astype(F32), pad], axis=0).astype(BF16)
        i = lax.broadcasted_iota(jnp.int32, (rows, LANES), 0) % t_new
        j = lax.broadcasted_iota(jnp.int32, (rows, LANES), 1)
        page(kn, vn, j < i)

    for i in reversed(range(g)):
        page(k_refs[i][...].astype(BF16), v_refs[i][...].astype(BF16), None)

    @pl.when(step == pl.num_programs(1) - 1)
    def _finish():
        lane = lax.broadcasted_iota(jnp.int32, (t_new, LANES), 1)
        for p in range(SB_WIDTH // LANES):
            cols = slice(p * LANES, (p + 1) * LANES)
            even = acc_ref[2 * p * t_new:(2 * p + 1) * t_new, cols]
            odd = acc_ref[(2 * p + 1) * t_new:(2 * p + 2) * t_new, cols]
            o_ref[:, cols] = jnp.where(lane < SB_HD, even, odd).astype(BF16)


def _sb_decode(q3, kn3, vn3, bias, cache_k, cache_v, page_table):
    b, t_new, _ = q3.shape
    n_pages = page_table.shape[1]
    page_size = cache_k.shape[1]
    assert page_size == LANES and t_new == SUBLANES
    g = DEC_PAGES_PER_STEP
    while n_pages % g:
        g //= 2
    n_steps = n_pages // g
    ck = cache_k.reshape(cache_k.shape[0], page_size, SB_WIDTH)
    cv = cache_v.reshape(cache_v.shape[0], page_size, SB_WIDTH)
    u2 = _suffix_matrix(page_size)
    rows = SB_HEADS * t_new
    new = pl.BlockSpec((None, t_new, SB_WIDTH), lambda bi, s, pt: (bi, 0, 0))

    def page_spec(i):
        return pl.BlockSpec((None, page_size, SB_WIDTH),
                            lambda bi, s, pt: (pt[bi, (n_steps - 1 - s) * g + i], 0, 0))

    grid_spec = pltpu.PrefetchScalarGridSpec(
        num_scalar_prefetch=1,
        grid=(b, n_steps),
        in_specs=[pl.BlockSpec(memory_space=pltpu.SMEM), new, new, new,
                  pl.BlockSpec(u2.shape, lambda bi, s, pt: (0, 0))]
                 + [page_spec(i) for i in range(g)] * 2,
        out_specs=new,
        scratch_shapes=[pltpu.VMEM((rows, SB_WIDTH), BF16), pltpu.VMEM((rows, LANES), F32),
                        pltpu.VMEM((rows, SB_WIDTH), F32), pltpu.VMEM((rows, LANES), F32)],
    )
    return pl.pallas_call(
        functools.partial(_sb_decode_kernel, n_pages_step=g),
        grid_spec=grid_spec,
        out_shape=jax.ShapeDtypeStruct(q3.shape, BF16),
        compiler_params=_params("parallel", "arbitrary"),
        name="sb_decode",
    )(page_table, bias, q3, kn3, vn3, u2, *([ck] * g), *([cv] * g))


def _ret_kernel(qr_ref, kr_ref, vr_ref, g_ref, s0_ref, dm_ref, qd_ref, kd_ref, cd_ref,
                o_ref, s_out_ref, s_scr, *, chunk):
    t = pl.program_id(1)
    tt = qr_ref.shape[0]
    pad = tt < chunk
    n_chunks = 1 if pad else tt // chunk

    @pl.when(t == 0)
    def _():
        s_scr[...] = s0_ref[...]

    def load(ref, rows, cols):
        x = ref[rows, cols]
        if pad:
            x = jnp.concatenate([x.astype(F32), jnp.zeros((chunk - tt, x.shape[1]), F32)], axis=0)
        return x

    for h in range(RET_HEADS):
        cols = slice(h * RET_DK, (h + 1) * RET_DK)
        s = s_scr[h]
        for c in range(n_chunks):
            rows = slice(0, tt) if pad else slice(c * chunk, (c + 1) * chunk)
            qc = load(qr_ref, rows, cols).astype(BF16)
            kc = load(kr_ref, rows, cols)
            vc = load(vr_ref, rows, cols).astype(BF16)
            att = lax.dot_general(qc, kc.astype(BF16), (((1,), (1,)), ((), ())),
                                  preferred_element_type=F32) * dm_ref[h]
            o = jnp.dot(att.astype(BF16), vc, preferred_element_type=F32)
            o = o + jnp.dot(qc, s.astype(BF16), preferred_element_type=F32) * qd_ref[h]
            kdec = (kc.astype(F32) * kd_ref[h]).astype(BF16)
            s = s * cd_ref[h] + lax.dot_general(kdec, vc, (((0,), (0,)), ((), ())),
                                                preferred_element_type=F32)
            mu = jnp.mean(o, axis=-1, keepdims=True)
            d = o - mu
            var = jnp.mean(d * d, axis=-1, keepdims=True)
            on = d * lax.rsqrt(var + LN_EPS)
            gate = g_ref[rows, cols]
            swish = gate * (1.0 / (1.0 + jnp.exp(-gate)))
            o_ref[rows, cols] = ((on[:tt] if pad else on) * swish).astype(BF16)
        s_scr[h] = s

    @pl.when(t == pl.num_programs(1) - 1)
    def _():
        s_out_ref[...] = s_scr[...]


def _retention_tables(chunk, true_chunk):
    lg = jnp.log(1.0 - 2.0 ** (-5.0 - jnp.arange(RET_HEADS, dtype=F32)))
    idx = jnp.arange(chunk, dtype=F32)
    diff = idx[:, None] - idx[None, :]
    dm = jnp.where(diff[None] >= 0, jnp.exp(jnp.maximum(diff, 0.0)[None] * lg[:, None, None]), 0.0)
    qd = jnp.exp((idx[None, :] + 1.0) * lg[:, None])
    kd = jnp.exp((true_chunk - 1.0 - idx)[None, :] * lg[:, None])
    kd = jnp.where(idx[None, :] < true_chunk, kd, 0.0)
    cd = jnp.exp(true_chunk * lg)
    rep = lambda a: jnp.broadcast_to(a[:, :, None], (RET_HEADS, chunk, RET_DK))
    cd = jnp.broadcast_to(cd[:, None, None], (RET_HEADS, SUBLANES, RET_DK))
    return dm, rep(qd), rep(kd), cd[:, :1, :]


def _retention(qr3, kr3, vr3, g3, s0):
    b, t, _ = qr3.shape
    chunk = RET_CHUNK
    true_chunk = chunk if t % chunk == 0 else t
    assert true_chunk <= chunk
    tt = min(ROW_TILE, t)
    dm, qd, kd, cd = _retention_tables(chunk, true_chunk)
    seq = pl.BlockSpec((None, tt, RET_WIDTH), lambda bi, ti: (bi, ti, 0))
    state = pl.BlockSpec((None,) + s0.shape[1:], lambda bi, ti: (bi, 0, 0, 0))
    return pl.pallas_call(
        functools.partial(_ret_kernel, chunk=chunk),
        grid=(b, t // tt),
        in_specs=[seq, seq, seq, seq, state] + [_resident(a.shape) for a in (dm, qd, kd, cd)],
        out_specs=[seq, state],
        out_shape=[jax.ShapeDtypeStruct(qr3.shape, BF16), jax.ShapeDtypeStruct(s0.shape, F32)],
        scratch_shapes=[pltpu.VMEM(s0.shape[1:], F32)],
        compiler_params=_params("parallel", "arbitrary"),
        name="retention",
    )(qr3, kr3, vr3, g3, s0, dm, qd, kd, cd)


def _mix_kernel(osb_ref, oret_ref, x_ref, wo_ref, g_ref, b_ref, h_ref):
    mix = jnp.dot(osb_ref[...], wo_ref[:SB_WIDTH, :], preferred_element_type=F32)
    mix = mix + jnp.dot(oret_ref[...], wo_ref[SB_WIDTH:, :], preferred_element_type=F32)
    h_ref[...] = _layer_norm(ALPHA * x_ref[...] + mix, g_ref[...], b_ref[...])


def _mix(o_sb, o_ret, x2, w_o_bf, ln_g, ln_b):
    n = x2.shape[0]
    tm = min(ROW_TILE, n)
    row = lambda i: (i, 0)
    return pl.pallas_call(
        _mix_kernel,
        grid=(n // tm,),
        in_specs=[pl.BlockSpec((tm, SB_WIDTH), row), pl.BlockSpec((tm, RET_WIDTH), row),
                  pl.BlockSpec((tm, D_MODEL), row), _resident(w_o_bf.shape),
                  _resident(ln_g.shape), _resident(ln_b.shape)],
        out_specs=pl.BlockSpec((tm, D_MODEL), row),
        out_shape=jax.ShapeDtypeStruct(x2.shape, F32),
        compiler_params=_params("parallel"),
        name="mix_ln1",
    )(o_sb, o_ret, x2, w_o_bf, ln_g, ln_b)


def _gelu_tanh(x):
    return x * (0.5 * (1.0 + jnp.tanh(0.7978845608028654 * (x + 0.044715 * (x * x * x)))))


def _ffn_kernel(h_ref, prev_ref, wup_ref, cw_ref, cb_ref, wdn_ref, g_ref, b_ref,
                y_ref, cs_ref, carry_scr, *, n_seq, use_carry):
    rows = h_ref.shape[0]
    tt = rows // n_seq
    if use_carry:
        @pl.when(pl.program_id(1) == 0)
        def _():
            carry_scr[...] = prev_ref[...]
        prev = carry_scr
    else:
        prev = prev_ref

    h = h_ref[...]
    hb = h.astype(BF16)
    cw = D_FF // FFN_CHUNKS
    tpos = lax.broadcasted_iota(jnp.int32, (n_seq, tt, cw), 1)

    def conv_cols(c0):
        u = jnp.dot(hb, wup_ref[:, c0:c0 + cw], preferred_element_type=F32)
        u3 = u.reshape(n_seq, tt, cw)
        p0 = prev[:, 0:1, c0:c0 + cw]
        p1 = prev[:, 1:2, c0:c0 + cw]
        r1 = pltpu.roll(u, 1, axis=0).reshape(n_seq, tt, cw)
        r2 = pltpu.roll(u, 2, axis=0).reshape(n_seq, tt, cw)
        um1 = jnp.where(tpos == 0, p1, r1)
        um2 = jnp.where(tpos == 0, p0, jnp.where(tpos == 1, p1, r2))
        w = cw_ref[:, c0:c0 + cw]
        c = cb_ref[:, c0:c0 + cw] + w[0:1] * um2 + w[1:2] * um1 + w[2:3] * u3
        last = u3[:, tt - 2:tt, :]
        cs_ref[:, :, c0:c0 + cw] = last
        if use_carry:
            carry_scr[:, :, c0:c0 + cw] = last
        return c.reshape(rows, cw)

    f = None
    for j in range(FFN_CHUNKS):
        ca = conv_cols(j * cw)
        cb = conv_cols(D_FF + j * cw)
        gated = (_gelu_tanh(ca) * cb).astype(BF16)
        part = jnp.dot(gated, wdn_ref[j * cw:(j + 1) * cw, :], preferred_element_type=F32)
        f = part if f is None else f + part
    y_ref[...] = _layer_norm(ALPHA * h + f, g_ref[...], b_ref[...])


def _ffn(h3, conv0, w_up_bf, conv_w, conv_b, w_down_bf, ln_g, ln_b):
    b, t, _ = h3.shape
    weights = [_resident(a.shape) for a in (w_up_bf, conv_w, conv_b, w_down_bf, ln_g, ln_b)]
    if t >= FFN_ROW_TILE:
        tm, n_seq = FFN_ROW_TILE, 1
        h_in = h3
        grid = (b, t // tm)
        hspec = pl.BlockSpec((None, tm, D_MODEL), lambda bi, ti: (bi, ti, 0))
        cspec = pl.BlockSpec((1, CONV_W - 1, 2 * D_FF), lambda bi, ti: (bi, 0, 0))
        y_shape = h3.shape
    else:
        n_seq = FFN_ROW_TILE // t
        while b % n_seq:
            n_seq //= 2
        tm = n_seq * t
        h_in = h3.reshape(b * t, D_MODEL)
        grid = (b // n_seq, 1)
        hspec = pl.BlockSpec((tm, D_MODEL), lambda bi, ti: (bi, 0))
        cspec = pl.BlockSpec((n_seq, CONV_W - 1, 2 * D_FF), lambda bi, ti: (bi, 0, 0))
        y_shape = h_in.shape
    y, cs = pl.pallas_call(
        functools.partial(_ffn_kernel, n_seq=n_seq, use_carry=t >= FFN_ROW_TILE),
        grid=grid,
        in_specs=[hspec, cspec] + weights,
        out_specs=[hspec, cspec],
        out_shape=[jax.ShapeDtypeStruct(y_shape, F32), jax.ShapeDtypeStruct(conv0.shape, F32)],
        scratch_shapes=[pltpu.VMEM((n_seq, CONV_W - 1, 2 * D_FF), F32)],
        compiler_params=_params("parallel", "arbitrary"),
        name="conv_ffn",
    )(h_in, conv0, w_up_bf, conv_w, conv_b, w_down_bf, ln_g, ln_b)
    return y.reshape(h3.shape), cs


def _rotation_tables(pos):
    half = RET_DK // 2
    inv = 1.0 / (ROT_BASE ** jnp.linspace(0.0, 1.0, half, dtype=F32))
    ang = pos.astype(F32)[:, None] * inv[None, :]
    cos, sin = jnp.cos(ang), jnp.sin(ang)
    return jnp.concatenate([cos, cos], axis=-1), jnp.concatenate([-sin, sin], axis=-1)


def _hybrid_layer(x, pos0, attend, s_ret0, conv_buf, w):
    b, t, _ = x.shape
    n = b * t
    x2 = x.reshape(n, D_MODEL)
    cos_t, sin_t = _rotation_tables(pos0 + jnp.arange(t))
    tm = min(ROW_TILE, n)
    if t < tm:
        cos_t, sin_t = jnp.tile(cos_t, (tm // t, 1)), jnp.tile(sin_t, (tm // t, 1))
    q, k, v, kb, vb, qr, kr, vr, gate = _proj(x2, w["w_in"], cos_t, sin_t)
    three = lambda a: a.reshape(b, t, a.shape[-1])
    o_sb = attend(three(q), three(kb), three(vb), w["sb_bias"])
    o_ret, s_new = _retention(three(qr), three(kr), three(vr), three(gate), s_ret0)
    h = _mix(o_sb.reshape(n, SB_WIDTH), o_ret.reshape(n, RET_WIDTH), x2,
             w["w_o"], w["ln1_g"], w["ln1_b"])
    y, conv_new = _ffn(three(h), conv_buf, w["w_up"], w["conv_w"], w["conv_b"], w["w_down"],
                       w["ln2_g"], w["ln2_b"])
    heads = lambda a: a.reshape(b, t, SB_HEADS, SB_HD)
    return y, heads(k), heads(v), s_new, conv_new


def kernel(x_prompt, x_sample, cache_k, cache_v, page_table, state_ret, state_conv, w_in, sb_bias,
           w_o, ln1_g, ln1_b, w_up, conv_w, conv_b, w_down, ln2_g, ln2_b):
    xp, xs = x_prompt, x_sample
    outs = [[] for _ in range(8)]
    for l in range(DEPTH):
        w = dict(w_in=w_in[l].astype(BF16), sb_bias=sb_bias[l], w_o=w_o[l].astype(BF16),
                 ln1_g=ln1_g[l][None], ln1_b=ln1_b[l][None], w_up=w_up[l].astype(BF16),
                 conv_w=conv_w[l], conv_b=conv_b[l][None], w_down=w_down[l].astype(BF16),
                 ln2_g=ln2_g[l][None], ln2_b=ln2_b[l][None])
        bp = xp.shape[0]
        s0 = jnp.zeros((bp, RET_HEADS, RET_DK, RET_DK), F32)
        c0 = jnp.zeros((bp, CONV_W - 1, 2 * D_FF), F32)
        xp, kp, vp, sp, cp = _hybrid_layer(xp, 0, _sb_prompt, s0, c0, w)
        past_len = page_table.shape[1] * cache_k.shape[2]
        attend = functools.partial(_sb_decode, cache_k=cache_k[l], cache_v=cache_v[l],
                                   page_table=page_table)
        xs, kk, vv, ss, cs = _hybrid_layer(xs, past_len, attend, state_ret[l], state_conv[l], w)
        for lst, a in zip(outs, (kp, vp, sp, cp, kk, vv, ss, cs)):
            lst.append(a)
    return (xp, xs) + tuple(jnp.stack(lst) for lst in outs)
```

```python
import functools

import jax
import jax.numpy as jnp
from jax import lax
from jax.experimental import pallas as pl
from jax.experimental.pallas import tpu as pltpu

F32 = jnp.float32
BF16 = jnp.bfloat16

D_MODEL = 1024
DEPTH = 1
SB_HD = 64
SB_WIDTH = D_MODEL // 2
SB_HEADS = SB_WIDTH // SB_HD
SB_SCALE = SB_HD ** -0.5
RET_HEADS = 4
RET_WIDTH = D_MODEL - SB_WIDTH
RET_DK = RET_WIDTH // RET_HEADS
RET_CHUNK = 128
ROT_BASE = 10000.0
PROJ_WIDTH = 3 * SB_WIDTH + 4 * RET_WIDTH
D_FF = ((8 * D_MODEL // 3 + 127) // 128) * 128
CONV_W = 3
ALPHA = (2.0 * DEPTH) ** 0.25
LN_EPS = 1e-5
LOG2E = 1.4426950408889634
MASK_BIAS = -1e30

LANES = 128
SUBLANES = 8
VMEM_LIMIT = 56 * 1024 * 1024

ROW_TILE = 512
FFN_ROW_TILE = 256
FFN_CHUNKS = 2
SB_TILE = 256
DEC_PAGES_PER_STEP = 8


def _params(*sem):
    return pltpu.CompilerParams(dimension_semantics=sem, vmem_limit_bytes=VMEM_LIMIT)


def _resident(shape):
    nd = len(shape)
    return pl.BlockSpec(shape, lambda *_: (0,) * nd, pipeline_mode=pl.Buffered(1))


def _layer_norm(x, g, b):
    mu = jnp.mean(x, axis=-1, keepdims=True)
    d = x - mu
    var = jnp.mean(d * d, axis=-1, keepdims=True)
    return d * lax.rsqrt(var + LN_EPS) * g + b


def _proj_kernel(x_ref, w_ref, cos_ref, sin_ref,
                 q_o, k_o, v_o, kb_o, vb_o, qr_o, kr_o, vr_o, g_o):
    xb = x_ref[...].astype(BF16)

    def mm(c0, width):
        return jnp.dot(xb, w_ref[:, c0:c0 + width], preferred_element_type=F32)

    q_o[...] = (mm(0, SB_WIDTH) * (SB_SCALE * LOG2E)).astype(BF16)
    k = mm(SB_WIDTH, SB_WIDTH)
    k_o[...] = k
    kb_o[...] = k.astype(BF16)
    v = mm(2 * SB_WIDTH, SB_WIDTH)
    v_o[...] = v
    vb_o[...] = v.astype(BF16)

    cos = cos_ref[...]
    sin = sin_ref[...]

    def rotate_into(out_ref, xr, scale):
        for h in range(RET_HEADS):
            xh = xr[:, h * RET_DK:(h + 1) * RET_DK]
            r = xh * cos + pltpu.roll(xh, RET_DK // 2, axis=1) * sin
            if scale is not None:
                r = r * scale
            out_ref[:, h * RET_DK:(h + 1) * RET_DK] = r.astype(BF16)

    base = 3 * SB_WIDTH
    rotate_into(qr_o, mm(base, RET_WIDTH), None)
    rotate_into(kr_o, mm(base + RET_WIDTH, RET_WIDTH), RET_DK ** -0.5)
    vr_o[...] = mm(base + 2 * RET_WIDTH, RET_WIDTH).astype(BF16)
    g_o[...] = mm(base + 3 * RET_WIDTH, RET_WIDTH)


def _proj(x2, w_in_bf, cos_t, sin_t):
    n = x2.shape[0]
    tm = min(ROW_TILE, n)
    n_tab = cos_t.shape[0] // tm
    row = lambda i: (i, 0)
    tab = lambda i: (i % n_tab, 0)
    half = lambda dt: jax.ShapeDtypeStruct((n, SB_WIDTH), dt)
    spec = pl.BlockSpec((tm, SB_WIDTH), row)
    return pl.pallas_call(
        _proj_kernel,
        grid=(n // tm,),
        in_specs=[pl.BlockSpec((tm, D_MODEL), row), _resident(w_in_bf.shape),
                  pl.BlockSpec((tm, RET_DK), tab), pl.BlockSpec((tm, RET_DK), tab)],
        out_specs=[spec] * 9,
        out_shape=[half(BF16), half(F32), half(F32), half(BF16), half(BF16),
                   half(BF16), half(BF16), half(BF16), half(F32)],
        compiler_params=_params("parallel"),
        name="proj",
    )(x2, w_in_bf, cos_t, sin_t)


def _neg_abs(x):
    bits = lax.bitcast_convert_type(x, jnp.uint32) | jnp.uint32(0x80000000)
    return lax.bitcast_convert_type(bits, F32)


def _log2_sigmoids(y):
    ls = jnp.minimum(y, 0.0) - jnp.log2(1.0 + jnp.exp2(_neg_abs(y)))
    return ls, ls - y


def _suffix_matrix(n):
    j = lax.broadcasted_iota(jnp.int32, (n, n), 0)
    s = lax.broadcasted_iota(jnp.int32, (n, n), 1)
    return (j > s).astype(BF16)


def _sb_prompt_kernel(bias_ref, q_ref, k_ref, v_ref, u_ref, o_ref, mb_ref, acc_ref, carry_ref,
                      s0, s1, ls0, ls1, hi0, hi1, tot0, tot1, w0, w1, *, tq):
    pair = pl.program_id(1)
    nq = q_ref.shape[0] // tq
    n_items = nq * (nq + 1) // 2
    n_lanes = tq // LANES
    lane = lax.broadcasted_iota(jnp.int32, (tq, LANES), 1)
    low = lane < SB_HD
    s_buf, ls_buf, hi_buf, tot_buf, w_buf = (s0, s1), (ls0, ls1), (hi0, hi1), (tot0, tot1), (w0, w1)

    row = lax.broadcasted_iota(jnp.int32, (tq, tq), 0)
    col = lax.broadcasted_iota(jnp.int32, (tq, tq), 1)
    neg = jnp.where(col < row, 0.0, MASK_BIAS)
    for h in range(2):
        b2 = bias_ref[2 * pair + h] * LOG2E
        mb_ref[h, 0] = neg + b2
        mb_ref[h, 1] = jnp.zeros((tq, tq), F32) + b2
    for ref in (acc_ref, carry_ref) + s_buf + ls_buf + hi_buf + tot_buf + w_buf:
        ref[...] = jnp.zeros_like(ref)

    def step(t, items, p):
        (qi0, j0), (_, j1), (_, j2), (qi3, j3) = items
        wr, rd = p, 1 - p

        def scores():
            q = q_ref[pl.ds(pl.multiple_of(qi0 * tq, tq), tq), :]
            k = k_ref[pl.ds(pl.multiple_of((qi0 - j0) * tq, tq), tq), :]
            zero = jnp.zeros_like(q)
            for h, qh in enumerate((jnp.where(low, q, zero), jnp.where(low, zero, q))):
                s_buf[wr][h] = lax.dot_general(qh, k, (((1,), (1,)), ((), ())),
                                               preferred_element_type=F32)

        def logits():
            diag = jnp.where(j1 == 0, 0, 1)
            for h in range(2):
                ls, lstay = _log2_sigmoids(s_buf[rd][h] + mb_ref[h, diag])
                ls_buf[wr][h] = ls
                hi_buf[wr][h] = lstay.astype(BF16)
                tot_buf[wr][h] = jnp.broadcast_to(jnp.sum(lstay, axis=1, keepdims=True),
                                                  (tq, LANES))

        def weights():
            for h in range(2):
                suf = jnp.dot(hi_buf[rd][h], u_ref[...], preferred_element_type=F32)
                c = carry_ref[h]
                cc = jnp.concatenate([c] * n_lanes, axis=1)
                w_buf[wr][h] = jnp.exp2(ls_buf[rd][h] + suf + cc).astype(BF16)
                carry_ref[h] = c + tot_buf[rd][h]

        def outputs():
            v = v_ref[pl.ds(pl.multiple_of((qi3 - j3) * tq, tq), tq), :]
            for h in range(2):
                acc_ref[h] += jnp.dot(w_buf[rd][h], v, preferred_element_type=F32)

        outputs()
        weights()
        logits()
        scores()

        @pl.when((t >= 3) & (j3 == qi3))
        def _():
            o_ref[pl.ds(pl.multiple_of(qi3 * tq, tq), tq), :] = (
                jnp.where(low, acc_ref[0], acc_ref[1]).astype(BF16))

        @pl.when(j2 == 0)
        def _():
            acc_ref[...] = jnp.zeros_like(acc_ref)

        @pl.when(j1 == 0)
        def _():
            carry_ref[...] = jnp.zeros_like(carry_ref)

        wrap = j0 == qi0
        done = wrap & (qi0 == nq - 1)
        nxt = (jnp.where(wrap & ~done, qi0 + 1, qi0), jnp.where(done, j0, jnp.where(wrap, 0, j0 + 1)))
        return (nxt,) + items[:3]

    def body(i, items):
        return step(2 * i + 1, step(2 * i, items, 0), 1)

    zero_item = (jnp.int32(0), jnp.int32(0))
    lax.fori_loop(0, (n_items + 3 + 1) // 2, body, (zero_item,) * 4)


def _sb_prompt(q3, new_rows, bias):
    k3, v3 = new_rows["k_bf16"], new_rows["v_bf16"]
    b, t, _ = q3.shape
    tq = SB_TILE
    n_pairs = SB_WIDTH // LANES
    u = _suffix_matrix(tq)
    spec = pl.BlockSpec((None, t, LANES), lambda bi, p: (bi, 0, p))
    f32 = lambda n: pltpu.VMEM((2, tq, n), F32)
    bf16 = lambda n: pltpu.VMEM((2, tq, n), BF16)
    return pl.pallas_call(
        functools.partial(_sb_prompt_kernel, tq=tq),
        grid=(b, n_pairs),
        in_specs=[pl.BlockSpec(memory_space=pltpu.SMEM), spec, spec, spec, _resident(u.shape)],
        out_specs=spec,
        out_shape=jax.ShapeDtypeStruct(q3.shape, BF16),
        scratch_shapes=[pltpu.VMEM((2, 2, tq, tq), F32), f32(LANES), f32(LANES),
                        f32(tq), f32(tq), f32(tq), f32(tq), bf16(tq), bf16(tq),
                        f32(LANES), f32(LANES), bf16(tq), bf16(tq)],
        compiler_params=_params("parallel", "parallel"),
        name="sb_prompt",
    )(bias, q3, k3, v3, u)


def _sb_decode_kernel(pt_ref, bias_ref, q_ref, kn_ref, vn_ref, u_ref, *rest, n_pages_step):
    del pt_ref
    g = n_pages_step
    k_refs, v_refs = rest[:g], rest[g:2 * g]
    o_ref, qh_ref, biasv_ref, acc_ref, carry_ref = rest[2 * g:]
    step = pl.program_id(1)
    t_new = q_ref.shape[0]
    rows_h = qh_ref.shape[1]
    page = u_ref.shape[0]
    contract_last = (((1,), (1,)), ((), ()))

    def head_rows(ref, h):
        n_keys = ref.shape[0] // SB_HEADS
        x = ref[pl.ds(h, n_keys, stride=SB_HEADS), :]
        if n_keys < page:
            x = jnp.concatenate([x, jnp.zeros((page - n_keys, SB_HD), F32)], axis=0)
        return x

    def head_keys(refs, h):
        return jnp.concatenate([head_rows(r, h) for r in refs], axis=0).astype(BF16)

    def attend(k_src, v_src, mask):
        n_blk = len(k_src)
        y = jnp.concatenate(
            [lax.dot_general(qh_ref[h], head_keys(k_src, h), contract_last,
                             preferred_element_type=F32) for h in range(SB_HEADS)], axis=0)
        logs = []
        for i in range(n_blk):
            yi = y[:, i * page:(i + 1) * page] + biasv_ref[...]
            if mask is not None:
                yi = jnp.where(mask, yi, MASK_BIAS)
            logs.append(_log2_sigmoids(yi))
        sufs = [jnp.dot(lstay.astype(BF16), u_ref[...], preferred_element_type=F32)
                for _, lstay in logs]
        carry = carry_ref[...]
        ws = [None] * n_blk
        for i in reversed(range(n_blk)):
            ls, lstay = logs[i]
            ws[i] = jnp.exp2(ls + sufs[i] + carry).astype(BF16)
            carry = carry + jnp.sum(lstay, axis=1, keepdims=True)
        carry_ref[...] = carry
        w = jnp.concatenate(ws, axis=1)
        for h in range(SB_HEADS):
            acc_ref[h] += jnp.dot(w[h * rows_h:(h + 1) * rows_h, :], head_keys(v_src, h),
                                  preferred_element_type=F32)

    @pl.when(step == 0)
    def _init():
        q = q_ref[...].astype(F32)
        pad = jnp.zeros((rows_h - t_new, SB_HD), F32)
        for h in range(SB_HEADS):
            qh = jnp.concatenate([q[:, h * SB_HD:(h + 1) * SB_HD], pad], axis=0)
            qh_ref[h] = qh.astype(BF16)
        r = lax.broadcasted_iota(jnp.int32, (SB_HEADS * rows_h, page), 0)
        bv = jnp.zeros((SB_HEADS * rows_h, page), F32)
        for h in range(SB_HEADS):
            bv = jnp.where(r // rows_h == h, bias_ref[h] * LOG2E, bv)
        biasv_ref[...] = bv
        acc_ref[...] = jnp.zeros_like(acc_ref)
        carry_ref[...] = jnp.zeros_like(carry_ref)
        j = lax.broadcasted_iota(jnp.int32, (SB_HEADS * rows_h, page), 1)
        attend([kn_ref], [vn_ref], j < r % rows_h)

    attend(k_refs, v_refs, None)

    @pl.when(step == pl.num_programs(1) - 1)
    def _finish():
        o_ref[...] = jnp.concatenate([acc_ref[h][:t_new] for h in range(SB_HEADS)],
                                     axis=1).astype(BF16)


def _sb_decode(q3, new_rows, bias, cache_k, cache_v, page_table):
    b, t_new, _ = q3.shape
    k_rows = new_rows["k"].reshape(b, t_new * SB_HEADS, SB_HD)
    v_rows = new_rows["v"].reshape(b, t_new * SB_HEADS, SB_HD)
    n_pages = page_table.shape[1]
    n_phys, page_size = cache_k.shape[:2]
    assert page_size == LANES and t_new == SUBLANES
    rows_h = 2 * SUBLANES
    g = DEC_PAGES_PER_STEP
    while n_pages % g:
        g //= 2
    n_steps = n_pages // g
    ck = cache_k.reshape(n_phys, page_size * SB_HEADS, SB_HD)
    cv = cache_v.reshape(n_phys, page_size * SB_HEADS, SB_HD)
    u = _suffix_matrix(page_size)
    seq = lambda rows, width: pl.BlockSpec((None, rows, width), lambda bi, s, pt: (bi, 0, 0))

    def page_spec(i):
        return pl.BlockSpec((None, page_size * SB_HEADS, SB_HD),
                            lambda bi, s, pt: (pt[bi, (n_steps - 1 - s) * g + i], 0, 0))

    grid_spec = pltpu.PrefetchScalarGridSpec(
        num_scalar_prefetch=1,
        grid=(b, n_steps),
        in_specs=[pl.BlockSpec(memory_space=pltpu.SMEM), seq(t_new, SB_WIDTH),
                  seq(t_new * SB_HEADS, SB_HD), seq(t_new * SB_HEADS, SB_HD),
                  pl.BlockSpec(u.shape, lambda bi, s, pt: (0, 0))]
                 + [page_spec(i) for i in range(g)] * 2,
        out_specs=seq(t_new, SB_WIDTH),
        scratch_shapes=[pltpu.VMEM((SB_HEADS, rows_h, SB_HD), BF16),
                        pltpu.VMEM((SB_HEADS * rows_h, page_size), F32),
                        pltpu.VMEM((SB_HEADS, rows_h, SB_HD), F32),
                        pltpu.VMEM((SB_HEADS * rows_h, page_size), F32)],
    )
    return pl.pallas_call(
        functools.partial(_sb_decode_kernel, n_pages_step=g),
        grid_spec=grid_spec,
        out_shape=jax.ShapeDtypeStruct(q3.shape, BF16),
        compiler_params=_params("parallel", "arbitrary"),
        name="sb_decode",
    )(page_table, bias, q3, k_rows, v_rows, u, *([ck] * g), *([cv] * g))


def _ret_kernel(qr_ref, kr_ref, vr_ref, g_ref, s0_ref, dm_ref, qd_ref, kd_ref, cd_ref,
                o_ref, s_out_ref, s_scr, *, chunk):
    t = pl.program_id(1)
    tt = qr_ref.shape[0]
    pad = tt < chunk
    n_chunks = 1 if pad else tt // chunk

    @pl.when(t == 0)
    def _():
        s_scr[...] = s0_ref[...]

    def load(ref, rows, cols):
        x = ref[rows, cols]
        if pad:
            x = jnp.concatenate([x.astype(F32), jnp.zeros((chunk - tt, x.shape[1]), F32)], axis=0)
        return x

    for h in range(RET_HEADS):
        cols = slice(h * RET_DK, (h + 1) * RET_DK)
        s = s_scr[h]
        for c in range(n_chunks):
            rows = slice(0, tt) if pad else slice(c * chunk, (c + 1) * chunk)
            qc = load(qr_ref, rows, cols).astype(BF16)
            kc = load(kr_ref, rows, cols)
            vc = load(vr_ref, rows, cols).astype(BF16)
            att = lax.dot_general(qc, kc.astype(BF16), (((1,), (1,)), ((), ())),
                                  preferred_element_type=F32) * dm_ref[h]
            o = jnp.dot(att.astype(BF16), vc, preferred_element_type=F32)
            o = o + jnp.dot(qc, s.astype(BF16), preferred_element_type=F32) * qd_ref[h]
            kdec = (kc.astype(F32) * kd_ref[h]).astype(BF16)
            s = s * cd_ref[h] + lax.dot_general(kdec, vc, (((0,), (0,)), ((), ())),
                                                preferred_element_type=F32)
            mu = jnp.mean(o, axis=-1, keepdims=True)
            d = o - mu
            var = jnp.mean(d * d, axis=-1, keepdims=True)
            on = d * lax.rsqrt(var + LN_EPS)
            gate = g_ref[rows, cols]
            swish = gate * (1.0 / (1.0 + jnp.exp(-gate)))
            o_ref[rows, cols] = ((on[:tt] if pad else on) * swish).astype(BF16)
        s_scr[h] = s

    @pl.when(t == pl.num_programs(1) - 1)
    def _():
        s_out_ref[...] = s_scr[...]


def _retention_tables(chunk, true_chunk):
    lg = jnp.log(1.0 - 2.0 ** (-5.0 - jnp.arange(RET_HEADS, dtype=F32)))
    idx = jnp.arange(chunk, dtype=F32)
    diff = idx[:, None] - idx[None, :]
    dm = jnp.where(diff[None] >= 0, jnp.exp(jnp.maximum(diff, 0.0)[None] * lg[:, None, None]), 0.0)
    qd = jnp.exp((idx[None, :] + 1.0) * lg[:, None])
    kd = jnp.exp((true_chunk - 1.0 - idx)[None, :] * lg[:, None])
    kd = jnp.where(idx[None, :] < true_chunk, kd, 0.0)
    cd = jnp.exp(true_chunk * lg)
    rep = lambda a: jnp.broadcast_to(a[:, :, None], (RET_HEADS, chunk, RET_DK))
    cd = jnp.broadcast_to(cd[:, None, None], (RET_HEADS, SUBLANES, RET_DK))
    return dm, rep(qd), rep(kd), cd[:, :1, :]


def _retention(qr3, kr3, vr3, g3, s0):
    b, t, _ = qr3.shape
    chunk = RET_CHUNK
    true_chunk = chunk if t % chunk == 0 else t
    assert true_chunk <= chunk
    tt = min(ROW_TILE, t)
    dm, qd, kd, cd = _retention_tables(chunk, true_chunk)
    seq = pl.BlockSpec((None, tt, RET_WIDTH), lambda bi, ti: (bi, ti, 0))
    state = pl.BlockSpec((None,) + s0.shape[1:], lambda bi, ti: (bi, 0, 0, 0))
    return pl.pallas_call(
        functools.partial(_ret_kernel, chunk=chunk),
        grid=(b, t // tt),
        in_specs=[seq, seq, seq, seq, state] + [_resident(a.shape) for a in (dm, qd, kd, cd)],
        out_specs=[seq, state],
        out_shape=[jax.ShapeDtypeStruct(qr3.shape, BF16), jax.ShapeDtypeStruct(s0.shape, F32)],
        scratch_shapes=[pltpu.VMEM(s0.shape[1:], F32)],
        compiler_params=_params("parallel", "arbitrary"),
        name="retention",
    )(qr3, kr3, vr3, g3, s0, dm, qd, kd, cd)


def _mix_kernel(osb_ref, oret_ref, x_ref, wo_ref, g_ref, b_ref, h_ref):
    mix = jnp.dot(osb_ref[...], wo_ref[:SB_WIDTH, :], preferred_element_type=F32)
    mix = mix + jnp.dot(oret_ref[...], wo_ref[SB_WIDTH:, :], preferred_element_type=F32)
    h_ref[...] = _layer_norm(ALPHA * x_ref[...] + mix, g_ref[...], b_ref[...])


def _mix(o_sb, o_ret, x2, w_o_bf, ln_g, ln_b):
    n = x2.shape[0]
    tm = min(ROW_TILE, n)
    row = lambda i: (i, 0)
    return pl.pallas_call(
        _mix_kernel,
        grid=(n // tm,),
        in_specs=[pl.BlockSpec((tm, SB_WIDTH), row), pl.BlockSpec((tm, RET_WIDTH), row),
                  pl.BlockSpec((tm, D_MODEL), row), _resident(w_o_bf.shape),
                  _resident(ln_g.shape), _resident(ln_b.shape)],
        out_specs=pl.BlockSpec((tm, D_MODEL), row),
        out_shape=jax.ShapeDtypeStruct(x2.shape, F32),
        compiler_params=_params("parallel"),
        name="mix_ln1",
    )(o_sb, o_ret, x2, w_o_bf, ln_g, ln_b)


def _gelu_tanh(x):
    return x * (0.5 * (1.0 + jnp.tanh(0.7978845608028654 * (x + 0.044715 * (x * x * x)))))


def _ffn_kernel(h_ref, prev_ref, wup_ref, cw_ref, cb_ref, wdn_ref, g_ref, b_ref,
                y_ref, cs_ref, carry_scr, *, n_seq, use_carry):
    rows = h_ref.shape[0]
    tt = rows // n_seq
    if use_carry:
        @pl.when(pl.program_id(1) == 0)
        def _():
            carry_scr[...] = prev_ref[...]
        prev = carry_scr
    else:
        prev = prev_ref

    h = h_ref[...]
    hb = h.astype(BF16)
    cw = D_FF // FFN_CHUNKS
    tpos = lax.broadcasted_iota(jnp.int32, (n_seq, tt, cw), 1)

    def conv_cols(c0):
        u = jnp.dot(hb, wup_ref[:, c0:c0 + cw], preferred_element_type=F32)
        u3 = u.reshape(n_seq, tt, cw)
        p0 = prev[:, 0:1, c0:c0 + cw]
        p1 = prev[:, 1:2, c0:c0 + cw]
        r1 = pltpu.roll(u, 1, axis=0).reshape(n_seq, tt, cw)
        r2 = pltpu.roll(u, 2, axis=0).reshape(n_seq, tt, cw)
        um1 = jnp.where(tpos == 0, p1, r1)
        um2 = jnp.where(tpos == 0, p0, jnp.where(tpos == 1, p1, r2))
        w = cw_ref[:, c0:c0 + cw]
        c = cb_ref[:, c0:c0 + cw] + w[0:1] * um2 + w[1:2] * um1 + w[2:3] * u3
        last = u3[:, tt - 2:tt, :]
        cs_ref[:, :, c0:c0 + cw] = last
        if use_carry:
            carry_scr[:, :, c0:c0 + cw] = last
        return c.reshape(rows, cw)

    f = None
    for j in range(FFN_CHUNKS):
        ca = conv_cols(j * cw)
        cb = conv_cols(D_FF + j * cw)
        gated = (_gelu_tanh(ca) * cb).astype(BF16)
        part = jnp.dot(gated, wdn_ref[j * cw:(j + 1) * cw, :], preferred_element_type=F32)
        f = part if f is None else f + part
    y_ref[...] = _layer_norm(ALPHA * h + f, g_ref[...], b_ref[...])


def _ffn(h3, conv0, w_up_bf, conv_w, conv_b, w_down_bf, ln_g, ln_b):
    b, t, _ = h3.shape
    weights = [_resident(a.shape) for a in (w_up_bf, conv_w, conv_b, w_down_bf, ln_g, ln_b)]
    if t >= FFN_ROW_TILE:
        tm, n_seq = FFN_ROW_TILE, 1
        h_in = h3
        grid = (b, t // tm)
        hspec = pl.BlockSpec((None, tm, D_MODEL), lambda bi, ti: (bi, ti, 0))
        cspec = pl.BlockSpec((1, CONV_W - 1, 2 * D_FF), lambda bi, ti: (bi, 0, 0))
        y_shape = h3.shape
    else:
        n_seq = FFN_ROW_TILE // t
        while b % n_seq:
            n_seq //= 2
        tm = n_seq * t
        h_in = h3.reshape(b * t, D_MODEL)
        grid = (b // n_seq, 1)
        hspec = pl.BlockSpec((tm, D_MODEL), lambda bi, ti: (bi, 0))
        cspec = pl.BlockSpec((n_seq, CONV_W - 1, 2 * D_FF), lambda bi, ti: (bi, 0, 0))
        y_shape = h_in.shape
    y, cs = pl.pallas_call(
        functools.partial(_ffn_kernel, n_seq=n_seq, use_carry=t >= FFN_ROW_TILE),
        grid=grid,
        in_specs=[hspec, cspec] + weights,
        out_specs=[hspec, cspec],
        out_shape=[jax.ShapeDtypeStruct(y_shape, F32), jax.ShapeDtypeStruct(conv0.shape, F32)],
        scratch_shapes=[pltpu.VMEM((n_seq, CONV_W - 1, 2 * D_FF), F32)],
        compiler_params=_params("parallel", "arbitrary"),
        name="conv_ffn",
    )(h_in, conv0, w_up_bf, conv_w, conv_b, w_down_bf, ln_g, ln_b)
    return y.reshape(h3.shape), cs


def _rotation_tables(pos):
    half = RET_DK // 2
    inv = 1.0 / (ROT_BASE ** jnp.linspace(0.0, 1.0, half, dtype=F32))
    ang = pos.astype(F32)[:, None] * inv[None, :]
    cos, sin = jnp.cos(ang), jnp.sin(ang)
    return jnp.concatenate([cos, cos], axis=-1), jnp.concatenate([-sin, sin], axis=-1)


def _hybrid_layer(x, pos0, attend, s_ret0, conv_buf, w):
    b, t, _ = x.shape
    n = b * t
    x2 = x.reshape(n, D_MODEL)
    cos_t, sin_t = _rotation_tables(pos0 + jnp.arange(t))
    tm = min(ROW_TILE, n)
    if t < tm:
        cos_t, sin_t = jnp.tile(cos_t, (tm // t, 1)), jnp.tile(sin_t, (tm // t, 1))
    q, k, v, kb, vb, qr, kr, vr, gate = _proj(x2, w["w_in"], cos_t, sin_t)
    three = lambda a: a.reshape(b, t, a.shape[-1])
    heads = lambda a: a.reshape(b, t, SB_HEADS, SB_HD)
    k, v = heads(k), heads(v)
    new_rows = dict(k=k, v=v, k_bf16=three(kb), v_bf16=three(vb))
    o_sb = attend(three(q), new_rows, w["sb_bias"])
    o_ret, s_new = _retention(three(qr), three(kr), three(vr), three(gate), s_ret0)
    h = _mix(o_sb.reshape(n, SB_WIDTH), o_ret.reshape(n, RET_WIDTH), x2,
             w["w_o"], w["ln1_g"], w["ln1_b"])
    y, conv_new = _ffn(three(h), conv_buf, w["w_up"], w["conv_w"], w["conv_b"], w["w_down"],
                       w["ln2_g"], w["ln2_b"])
    return y, k, v, s_new, conv_new


def kernel(x_prompt, x_sample, cache_k, cache_v, page_table, state_ret, state_conv, w_in, sb_bias,
           w_o, ln1_g, ln1_b, w_up, conv_w, conv_b, w_down, ln2_g, ln2_b):
    xp, xs = x_prompt, x_sample
    outs = [[] for _ in range(8)]
    for l in range(DEPTH):
        w = dict(w_in=w_in[l].astype(BF16), sb_bias=sb_bias[l], w_o=w_o[l].astype(BF16),
                 ln1_g=ln1_g[l][None], ln1_b=ln1_b[l][None], w_up=w_up[l].astype(BF16),
                 conv_w=conv_w[l], conv_b=conv_b[l][None], w_down=w_down[l].astype(BF16),
                 ln2_g=ln2_g[l][None], ln2_b=ln2_b[l][None])
        bp = xp.shape[0]
        s0 = jnp.zeros((bp, RET_HEADS, RET_DK, RET_DK), F32)
        c0 = jnp.zeros((bp, CONV_W - 1, 2 * D_FF), F32)
        xp, kp, vp, sp, cp = _hybrid_layer(xp, 0, _sb_prompt, s0, c0, w)
        past_len = page_table.shape[1] * cache_k.shape[2]
        attend = functools.partial(_sb_decode, cache_k=cache_k[l], cache_v=cache_v[l],
                                   page_table=page_table)
        xs, kk, vv, ss, cs = _hybrid_layer(xs, past_len, attend, state_ret[l], state_conv[l], w)
        for lst, a in zip(outs, (kp, vp, sp, cp, kk, vv, ss, cs)):
            lst.append(a)
    return (xp, xs) + tuple(jnp.stack(lst) for lst in outs)
```

```python
import functools

import jax
import jax.numpy as jnp
from jax import lax
from jax.experimental import pallas as pl
from jax.experimental.pallas import tpu as pltpu

F32 = jnp.float32
BF16 = jnp.bfloat16

D_MODEL = 1024
DEPTH = 1
SB_HD = 64
SB_WIDTH = D_MODEL // 2
SB_HEADS = SB_WIDTH // SB_HD
SB_SCALE = SB_HD ** -0.5
RET_HEADS = 4
RET_WIDTH = D_MODEL - SB_WIDTH
RET_DK = RET_WIDTH // RET_HEADS
RET_CHUNK = 128
ROT_BASE = 10000.0
PROJ_WIDTH = 3 * SB_WIDTH + 4 * RET_WIDTH
D_FF = ((8 * D_MODEL // 3 + 127) // 128) * 128
CONV_W = 3
ALPHA = (2.0 * DEPTH) ** 0.25
LN_EPS = 1e-5
LOG2E = 1.4426950408889634
MASK_BIAS = -1e30

LANES = 128
SUBLANES = 8
VMEM_LIMIT = 56 * 1024 * 1024

ROW_TILE = 512
FFN_ROW_TILE = 256
FFN_CHUNKS = 2
SB_TILE = 256
DEC_PAGES_PER_STEP = 8


def _params(*sem):
    return pltpu.CompilerParams(dimension_semantics=sem, vmem_limit_bytes=VMEM_LIMIT)


def _resident(shape):
    nd = len(shape)
    return pl.BlockSpec(shape, lambda *_: (0,) * nd, pipeline_mode=pl.Buffered(1))


def _layer_norm(x, g, b):
    mu = jnp.mean(x, axis=-1, keepdims=True)
    d = x - mu
    var = jnp.mean(d * d, axis=-1, keepdims=True)
    return d * lax.rsqrt(var + LN_EPS) * g + b


def _proj_kernel(x_ref, w_ref, cos_ref, sin_ref,
                 q_o, k_o, v_o, kb_o, vb_o, qr_o, kr_o, vr_o, g_o):
    xb = x_ref[...].astype(BF16)

    def mm(c0, width):
        return jnp.dot(xb, w_ref[:, c0:c0 + width], preferred_element_type=F32)

    q_o[...] = (mm(0, SB_WIDTH) * (SB_SCALE * LOG2E)).astype(BF16)
    k = mm(SB_WIDTH, SB_WIDTH)
    k_o[...] = k
    kb_o[...] = k.astype(BF16)
    v = mm(2 * SB_WIDTH, SB_WIDTH)
    v_o[...] = v
    vb_o[...] = v.astype(BF16)

    cos = cos_ref[...]
    sin = sin_ref[...]

    def rotate_into(out_ref, xr, scale):
        for h in range(RET_HEADS):
            xh = xr[:, h * RET_DK:(h + 1) * RET_DK]
            r = xh * cos + pltpu.roll(xh, RET_DK // 2, axis=1) * sin
            if scale is not None:
                r = r * scale
            out_ref[:, h * RET_DK:(h + 1) * RET_DK] = r.astype(BF16)

    base = 3 * SB_WIDTH
    rotate_into(qr_o, mm(base, RET_WIDTH), None)
    rotate_into(kr_o, mm(base + RET_WIDTH, RET_WIDTH), RET_DK ** -0.5)
    vr_o[...] = mm(base + 2 * RET_WIDTH, RET_WIDTH).astype(BF16)
    g_o[...] = mm(base + 3 * RET_WIDTH, RET_WIDTH)


def _proj(x2, w_in_bf, cos_t, sin_t):
    n = x2.shape[0]
    tm = min(ROW_TILE, n)
    n_tab = cos_t.shape[0] // tm
    row = lambda i: (i, 0)
    tab = lambda i: (i % n_tab, 0)
    half = lambda dt: jax.ShapeDtypeStruct((n, SB_WIDTH), dt)
    spec = pl.BlockSpec((tm, SB_WIDTH), row)
    return pl.pallas_call(
        _proj_kernel,
        grid=(n // tm,),
        in_specs=[pl.BlockSpec((tm, D_MODEL), row), _resident(w_in_bf.shape),
                  pl.BlockSpec((tm, RET_DK), tab), pl.BlockSpec((tm, RET_DK), tab)],
        out_specs=[spec] * 9,
        out_shape=[half(BF16), half(F32), half(F32), half(BF16), half(BF16),
                   half(BF16), half(BF16), half(BF16), half(F32)],
        compiler_params=_params("parallel"),
        name="proj",
    )(x2, w_in_bf, cos_t, sin_t)


def _neg_abs(x):
    bits = lax.bitcast_convert_type(x, jnp.uint32) | jnp.uint32(0x80000000)
    return lax.bitcast_convert_type(bits, F32)


def _log2_sigmoids(y):
    ls = jnp.minimum(y, 0.0) - jnp.log2(1.0 + jnp.exp2(_neg_abs(y)))
    return ls, ls - y


def _suffix_matrix(n):
    j = lax.broadcasted_iota(jnp.int32, (n, n), 0)
    s = lax.broadcasted_iota(jnp.int32, (n, n), 1)
    return (j > s).astype(BF16)


def _sb_prompt_kernel(bias_ref, q_ref, k_ref, v_ref, u_ref, o_ref, mb_ref, acc_ref, carry_ref,
                      s0, s1, ls0, ls1, hi0, hi1, tot0, tot1, w0, w1, *, tq):
    pair = pl.program_id(1)
    nq = q_ref.shape[0] // tq
    n_items = nq * (nq + 1) // 2
    n_lanes = tq // LANES
    lane = lax.broadcasted_iota(jnp.int32, (tq, LANES), 1)
    low = lane < SB_HD
    s_buf, ls_buf, hi_buf, tot_buf, w_buf = (s0, s1), (ls0, ls1), (hi0, hi1), (tot0, tot1), (w0, w1)

    row = lax.broadcasted_iota(jnp.int32, (tq, tq), 0)
    col = lax.broadcasted_iota(jnp.int32, (tq, tq), 1)
    neg = jnp.where(col < row, 0.0, MASK_BIAS)
    for h in range(2):
        b2 = bias_ref[2 * pair + h] * LOG2E
        mb_ref[h, 0] = neg + b2
        mb_ref[h, 1] = jnp.zeros((tq, tq), F32) + b2
    for ref in (acc_ref, carry_ref) + s_buf + ls_buf + hi_buf + tot_buf + w_buf:
        ref[...] = jnp.zeros_like(ref)

    def step(t, items, p):
        (qi0, j0), (_, j1), (_, j2), (qi3, j3) = items
        wr, rd = p, 1 - p

        def scores():
            q = q_ref[pl.ds(pl.multiple_of(qi0 * tq, tq), tq), :]
            k = k_ref[pl.ds(pl.multiple_of((qi0 - j0) * tq, tq), tq), :]
            zero = jnp.zeros_like(q)
            for h, qh in enumerate((jnp.where(low, q, zero), jnp.where(low, zero, q))):
                s_buf[wr][h] = lax.dot_general(qh, k, (((1,), (1,)), ((), ())),
                                               preferred_element_type=F32)

        def logits():
            diag = jnp.where(j1 == 0, 0, 1)
            for h in range(2):
                ls, lstay = _log2_sigmoids(s_buf[rd][h] + mb_ref[h, diag])
                ls_buf[wr][h] = ls
                hi_buf[wr][h] = lstay.astype(BF16)
                tot_buf[wr][h] = jnp.broadcast_to(jnp.sum(lstay, axis=1, keepdims=True),
                                                  (tq, LANES))

        def weights():
            for h in range(2):
                suf = jnp.dot(hi_buf[rd][h], u_ref[...], preferred_element_type=F32)
                c = carry_ref[h]
                cc = jnp.concatenate([c] * n_lanes, axis=1)
                w_buf[wr][h] = jnp.exp2(ls_buf[rd][h] + suf + cc).astype(BF16)
                carry_ref[h] = c + tot_buf[rd][h]

        def outputs():
            v = v_ref[pl.ds(pl.multiple_of((qi3 - j3) * tq, tq), tq), :]
            for h in range(2):
                acc_ref[h] += jnp.dot(w_buf[rd][h], v, preferred_element_type=F32)

        outputs()
        weights()
        logits()
        scores()

        @pl.when((t >= 3) & (j3 == qi3))
        def _():
            o_ref[pl.ds(pl.multiple_of(qi3 * tq, tq), tq), :] = (
                jnp.where(low, acc_ref[0], acc_ref[1]).astype(BF16))

        @pl.when(j2 == 0)
        def _():
            acc_ref[...] = jnp.zeros_like(acc_ref)

        @pl.when(j1 == 0)
        def _():
            carry_ref[...] = jnp.zeros_like(carry_ref)

        wrap = j0 == qi0
        done = wrap & (qi0 == nq - 1)
        nxt = (jnp.where(wrap & ~done, qi0 + 1, qi0), jnp.where(done, j0, jnp.where(wrap, 0, j0 + 1)))
        return (nxt,) + items[:3]

    def body(i, items):
        return step(2 * i + 1, step(2 * i, items, 0), 1)

    zero_item = (jnp.int32(0), jnp.int32(0))
    lax.fori_loop(0, (n_items + 3 + 1) // 2, body, (zero_item,) * 4)


def _sb_prompt(q3, k3, v3, bias):
    b, t, _ = q3.shape
    tq = SB_TILE
    n_pairs = SB_WIDTH // LANES
    u = _suffix_matrix(tq)
    spec = pl.BlockSpec((None, t, LANES), lambda bi, p: (bi, 0, p))
    f32 = lambda n: pltpu.VMEM((2, tq, n), F32)
    bf16 = lambda n: pltpu.VMEM((2, tq, n), BF16)
    return pl.pallas_call(
        functools.partial(_sb_prompt_kernel, tq=tq),
        grid=(b, n_pairs),
        in_specs=[pl.BlockSpec(memory_space=pltpu.SMEM), spec, spec, spec, _resident(u.shape)],
        out_specs=spec,
        out_shape=jax.ShapeDtypeStruct(q3.shape, BF16),
        scratch_shapes=[pltpu.VMEM((2, 2, tq, tq), F32), f32(LANES), f32(LANES),
                        f32(tq), f32(tq), f32(tq), f32(tq), bf16(tq), bf16(tq),
                        f32(LANES), f32(LANES), bf16(tq), bf16(tq)],
        compiler_params=_params("parallel", "parallel"),
        name="sb_prompt",
    )(bias, q3, k3, v3, u)


def _sb_decode_kernel(pt_ref, bias_ref, q_ref, kn_ref, vn_ref, u_ref, *rest, n_pages_step):
    del pt_ref
    g = n_pages_step
    k_refs, v_refs = rest[:g], rest[g:2 * g]
    o_ref, qbd_ref, biasv_ref, acc_ref, carry_ref = rest[2 * g:]
    step = pl.program_id(1)
    t_new = q_ref.shape[0]
    rows, page = biasv_ref.shape
    contract_last = (((1,), (1,)), ((), ()))

    def weights(y, mask):
        n_blk = y.shape[1] // page
        logs = []
        for i in range(n_blk):
            yi = y[:, i * page:(i + 1) * page] + biasv_ref[...]
            if mask is not None:
                yi = jnp.where(mask, yi, MASK_BIAS)
            logs.append(_log2_sigmoids(yi))
        stay = jnp.concatenate([lstay.astype(BF16) for _, lstay in logs], axis=0)
        suf = jnp.dot(stay, u_ref[...], preferred_element_type=F32)
        carry = carry_ref[...]
        ws = [None] * n_blk
        for i in reversed(range(n_blk)):
            ls, lstay = logs[i]
            ws[i] = jnp.exp2(ls + suf[i * rows:(i + 1) * rows] + carry).astype(BF16)
            carry = carry + jnp.sum(lstay, axis=1, keepdims=True)
        carry_ref[...] = carry
        return jnp.concatenate(ws, axis=1)

    @pl.when(step == 0)
    def _init():
        q = q_ref[...].astype(F32)
        q_rep = jnp.concatenate([q] * SB_HEADS, axis=0)
        r = lax.broadcasted_iota(jnp.int32, (rows, SB_WIDTH), 0) // t_new
        c = lax.broadcasted_iota(jnp.int32, (rows, SB_WIDTH), 1) // SB_HD
        qbd_ref[...] = jnp.where(r == c, q_rep, 0.0).astype(BF16)
        rh = lax.broadcasted_iota(jnp.int32, (rows, page), 0) // t_new
        bv = jnp.zeros((rows, page), F32)
        for h in range(SB_HEADS):
            bv = jnp.where(rh == h, bias_ref[h] * LOG2E, bv)
        biasv_ref[...] = bv
        acc_ref[...] = jnp.zeros_like(acc_ref)
        carry_ref[...] = jnp.zeros_like(carry_ref)
        pad = jnp.zeros((page - t_new, SB_WIDTH), F32)
        kn = jnp.concatenate([kn_ref[...].astype(F32), pad], axis=0).astype(BF16)
        vn = jnp.concatenate([vn_ref[...].astype(F32), pad], axis=0).astype(BF16)
        i = lax.broadcasted_iota(jnp.int32, (rows, page), 0) % t_new
        j = lax.broadcasted_iota(jnp.int32, (rows, page), 1)
        y = lax.dot_general(qbd_ref[...], kn, contract_last, preferred_element_type=F32)
        acc_ref[...] += jnp.dot(weights(y, j < i), vn, preferred_element_type=F32)

    k_t = jnp.concatenate([r[...] for r in k_refs], axis=1).astype(BF16)
    y = jnp.dot(qbd_ref[...], k_t, preferred_element_type=F32)
    w = weights(y, None)
    v_t = jnp.concatenate([r[...] for r in v_refs], axis=1).astype(BF16)
    acc_ref[...] += lax.dot_general(w, v_t, contract_last, preferred_element_type=F32)

    @pl.when(step == pl.num_programs(1) - 1)
    def _finish():
        lane = lax.broadcasted_iota(jnp.int32, (t_new, LANES), 1)
        for p in range(SB_WIDTH // LANES):
            cols = slice(p * LANES, (p + 1) * LANES)
            even = acc_ref[2 * p * t_new:(2 * p + 1) * t_new, cols]
            odd = acc_ref[(2 * p + 1) * t_new:(2 * p + 2) * t_new, cols]
            o_ref[:, cols] = jnp.where(lane < SB_HD, even, odd).astype(BF16)


def _sb_decode(q3, kn3, vn3, bias, cache_k, cache_v, page_table):
    b, t_new, _ = q3.shape
    n_pages = page_table.shape[1]
    n_phys, page_size = cache_k.shape[:2]
    assert page_size == LANES and t_new == SUBLANES
    g = DEC_PAGES_PER_STEP
    while n_pages % g:
        g //= 2
    n_steps = n_pages // g
    ck = jnp.transpose(cache_k, (0, 2, 3, 1)).reshape(n_phys, SB_WIDTH, page_size)
    cv = jnp.transpose(cache_v, (0, 2, 3, 1)).reshape(n_phys, SB_WIDTH, page_size)
    u = _suffix_matrix(page_size)
    rows = SB_HEADS * t_new
    new = pl.BlockSpec((None, t_new, SB_WIDTH), lambda bi, s, pt: (bi, 0, 0))

    def page_spec(i):
        return pl.BlockSpec((None, SB_WIDTH, page_size),
                            lambda bi, s, pt: (pt[bi, (n_steps - 1 - s) * g + i], 0, 0))

    grid_spec = pltpu.PrefetchScalarGridSpec(
        num_scalar_prefetch=1,
        grid=(b, n_steps),
        in_specs=[pl.BlockSpec(memory_space=pltpu.SMEM), new, new, new,
                  pl.BlockSpec(u.shape, lambda bi, s, pt: (0, 0))]
                 + [page_spec(i) for i in range(g)] * 2,
        out_specs=new,
        scratch_shapes=[pltpu.VMEM((rows, SB_WIDTH), BF16), pltpu.VMEM((rows, page_size), F32),
                        pltpu.VMEM((rows, SB_WIDTH), F32), pltpu.VMEM((rows, page_size), F32)],
    )
    return pl.pallas_call(
        functools.partial(_sb_decode_kernel, n_pages_step=g),
        grid_spec=grid_spec,
        out_shape=jax.ShapeDtypeStruct(q3.shape, BF16),
        compiler_params=_params("parallel", "arbitrary"),
        name="sb_decode",
    )(page_table, bias, q3, kn3, vn3, u, *([ck] * g), *([cv] * g))


def _ret_kernel(qr_ref, kr_ref, vr_ref, g_ref, s0_ref, dm_ref, qd_ref, kd_ref, cd_ref,
                o_ref, s_out_ref, s_scr, *, chunk):
    t = pl.program_id(1)
    tt = qr_ref.shape[0]
    pad = tt < chunk
    n_chunks = 1 if pad else tt // chunk

    @pl.when(t == 0)
    def _():
        s_scr[...] = s0_ref[...]

    def load(ref, rows, cols):
        x = ref[rows, cols]
        if pad:
            x = jnp.concatenate([x.astype(F32), jnp.zeros((chunk - tt, x.shape[1]), F32)], axis=0)
        return x

    for h in range(RET_HEADS):
        cols = slice(h * RET_DK, (h + 1) * RET_DK)
        s = s_scr[h]
        for c in range(n_chunks):
            rows = slice(0, tt) if pad else slice(c * chunk, (c + 1) * chunk)
            qc = load(qr_ref, rows, cols).astype(BF16)
            kc = load(kr_ref, rows, cols)
            vc = load(vr_ref, rows, cols).astype(BF16)
            att = lax.dot_general(qc, kc.astype(BF16), (((1,), (1,)), ((), ())),
                                  preferred_element_type=F32) * dm_ref[h]
            o = jnp.dot(att.astype(BF16), vc, preferred_element_type=F32)
            o = o + jnp.dot(qc, s.astype(BF16), preferred_element_type=F32) * qd_ref[h]
            kdec = (kc.astype(F32) * kd_ref[h]).astype(BF16)
            s = s * cd_ref[h] + lax.dot_general(kdec, vc, (((0,), (0,)), ((), ())),
                                                preferred_element_type=F32)
            mu = jnp.mean(o, axis=-1, keepdims=True)
            d = o - mu
            var = jnp.mean(d * d, axis=-1, keepdims=True)
            on = d * lax.rsqrt(var + LN_EPS)
            gate = g_ref[rows, cols]
            swish = gate * (1.0 / (1.0 + jnp.exp(-gate)))
            o_ref[rows, cols] = ((on[:tt] if pad else on) * swish).astype(BF16)
        s_scr[h] = s

    @pl.when(t == pl.num_programs(1) - 1)
    def _():
        s_out_ref[...] = s_scr[...]


def _retention_tables(chunk, true_chunk):
    lg = jnp.log(1.0 - 2.0 ** (-5.0 - jnp.arange(RET_HEADS, dtype=F32)))
    idx = jnp.arange(chunk, dtype=F32)
    diff = idx[:, None] - idx[None, :]
    dm = jnp.where(diff[None] >= 0, jnp.exp(jnp.maximum(diff, 0.0)[None] * lg[:, None, None]), 0.0)
    qd = jnp.exp((idx[None, :] + 1.0) * lg[:, None])
    kd = jnp.exp((true_chunk - 1.0 - idx)[None, :] * lg[:, None])
    kd = jnp.where(idx[None, :] < true_chunk, kd, 0.0)
    cd = jnp.exp(true_chunk * lg)
    rep = lambda a: jnp.broadcast_to(a[:, :, None], (RET_HEADS, chunk, RET_DK))
    cd = jnp.broadcast_to(cd[:, None, None], (RET_HEADS, SUBLANES, RET_DK))
    return dm, rep(qd), rep(kd), cd[:, :1, :]


def _retention(qr3, kr3, vr3, g3, s0):
    b, t, _ = qr3.shape
    chunk = RET_CHUNK
    true_chunk = chunk if t % chunk == 0 else t
    assert true_chunk <= chunk
    tt = min(ROW_TILE, t)
    dm, qd, kd, cd = _retention_tables(chunk, true_chunk)
    seq = pl.BlockSpec((None, tt, RET_WIDTH), lambda bi, ti: (bi, ti, 0))
    state = pl.BlockSpec((None,) + s0.shape[1:], lambda bi, ti: (bi, 0, 0, 0))
    return pl.pallas_call(
        functools.partial(_ret_kernel, chunk=chunk),
        grid=(b, t // tt),
        in_specs=[seq, seq, seq, seq, state] + [_resident(a.shape) for a in (dm, qd, kd, cd)],
        out_specs=[seq, state],
        out_shape=[jax.ShapeDtypeStruct(qr3.shape, BF16), jax.ShapeDtypeStruct(s0.shape, F32)],
        scratch_shapes=[pltpu.VMEM(s0.shape[1:], F32)],
        compiler_params=_params("parallel", "arbitrary"),
        name="retention",
    )(qr3, kr3, vr3, g3, s0, dm, qd, kd, cd)


def _mix_kernel(osb_ref, oret_ref, x_ref, wo_ref, g_ref, b_ref, h_ref):
    mix = jnp.dot(osb_ref[...], wo_ref[:SB_WIDTH, :], preferred_element_type=F32)
    mix = mix + jnp.dot(oret_ref[...], wo_ref[SB_WIDTH:, :], preferred_element_type=F32)
    h_ref[...] = _layer_norm(ALPHA * x_ref[...] + mix, g_ref[...], b_ref[...])


def _mix(o_sb, o_ret, x2, w_o_bf, ln_g, ln_b):
    n = x2.shape[0]
    tm = min(ROW_TILE, n)
    row = lambda i: (i, 0)
    return pl.pallas_call(
        _mix_kernel,
        grid=(n // tm,),
        in_specs=[pl.BlockSpec((tm, SB_WIDTH), row), pl.BlockSpec((tm, RET_WIDTH), row),
                  pl.BlockSpec((tm, D_MODEL), row), _resident(w_o_bf.shape),
                  _resident(ln_g.shape), _resident(ln_b.shape)],
        out_specs=pl.BlockSpec((tm, D_MODEL), row),
        out_shape=jax.ShapeDtypeStruct(x2.shape, F32),
        compiler_params=_params("parallel"),
        name="mix_ln1",
    )(o_sb, o_ret, x2, w_o_bf, ln_g, ln_b)


def _gelu_tanh(x):
    return x * (0.5 * (1.0 + jnp.tanh(0.7978845608028654 * (x + 0.044715 * (x * x * x)))))


def _ffn_kernel(h_ref, prev_ref, wup_ref, cw_ref, cb_ref, wdn_ref, g_ref, b_ref,
                y_ref, cs_ref, carry_scr, *, n_seq, use_carry):
    rows = h_ref.shape[0]
    tt = rows // n_seq
    if use_carry:
        @pl.when(pl.program_id(1) == 0)
        def _():
            carry_scr[...] = prev_ref[...]
        prev = carry_scr
    else:
        prev = prev_ref

    h = h_ref[...]
    hb = h.astype(BF16)
    cw = D_FF // FFN_CHUNKS
    tpos = lax.broadcasted_iota(jnp.int32, (n_seq, tt, cw), 1)

    def conv_cols(c0):
        u = jnp.dot(hb, wup_ref[:, c0:c0 + cw], preferred_element_type=F32)
        u3 = u.reshape(n_seq, tt, cw)
        p0 = prev[:, 0:1, c0:c0 + cw]
        p1 = prev[:, 1:2, c0:c0 + cw]
        r1 = pltpu.roll(u, 1, axis=0).reshape(n_seq, tt, cw)
        r2 = pltpu.roll(u, 2, axis=0).reshape(n_seq, tt, cw)
        um1 = jnp.where(tpos == 0, p1, r1)
        um2 = jnp.where(tpos == 0, p0, jnp.where(tpos == 1, p1, r2))
        w = cw_ref[:, c0:c0 + cw]
        c = cb_ref[:, c0:c0 + cw] + w[0:1] * um2 + w[1:2] * um1 + w[2:3] * u3
        last = u3[:, tt - 2:tt, :]
        cs_ref[:, :, c0:c0 + cw] = last
        if use_carry:
            carry_scr[:, :, c0:c0 + cw] = last
        return c.reshape(rows, cw)

    f = None
    for j in range(FFN_CHUNKS):
        ca = conv_cols(j * cw)
        cb = conv_cols(D_FF + j * cw)
        gated = (_gelu_tanh(ca) * cb).astype(BF16)
        part = jnp.dot(gated, wdn_ref[j * cw:(j + 1) * cw, :], preferred_element_type=F32)
        f = part if f is None else f + part
    y_ref[...] = _layer_norm(ALPHA * h + f, g_ref[...], b_ref[...])


def _ffn(h3, conv0, w_up_bf, conv_w, conv_b, w_down_bf, ln_g, ln_b):
    b, t, _ = h3.shape
    weights = [_resident(a.shape) for a in (w_up_bf, conv_w, conv_b, w_down_bf, ln_g, ln_b)]
    if t >= FFN_ROW_TILE:
        tm, n_seq = FFN_ROW_TILE, 1
        h_in = h3
        grid = (b, t // tm)
        hspec = pl.BlockSpec((None, tm, D_MODEL), lambda bi, ti: (bi, ti, 0))
        cspec = pl.BlockSpec((1, CONV_W - 1, 2 * D_FF), lambda bi, ti: (bi, 0, 0))
        y_shape = h3.shape
    else:
        n_seq = FFN_ROW_TILE // t
        while b % n_seq:
            n_seq //= 2
        tm = n_seq * t
        h_in = h3.reshape(b * t, D_MODEL)
        grid = (b // n_seq, 1)
        hspec = pl.BlockSpec((tm, D_MODEL), lambda bi, ti: (bi, 0))
        cspec = pl.BlockSpec((n_seq, CONV_W - 1, 2 * D_FF), lambda bi, ti: (bi, 0, 0))
        y_shape = h_in.shape
    y, cs = pl.pallas_call(
        functools.partial(_ffn_kernel, n_seq=n_seq, use_carry=t >= FFN_ROW_TILE),
        grid=grid,
        in_specs=[hspec, cspec] + weights,
        out_specs=[hspec, cspec],
        out_shape=[jax.ShapeDtypeStruct(y_shape, F32), jax.ShapeDtypeStruct(conv0.shape, F32)],
        scratch_shapes=[pltpu.VMEM((n_seq, CONV_W - 1, 2 * D_FF), F32)],
        compiler_params=_params("parallel", "arbitrary"),
        name="conv_ffn",
    )(h_in, conv0, w_up_bf, conv_w, conv_b, w_down_bf, ln_g, ln_b)
    return y.reshape(h3.shape), cs


def _rotation_tables(pos):
    half = RET_DK // 2
    inv = 1.0 / (ROT_BASE ** jnp.linspace(0.0, 1.0, half, dtype=F32))
    ang = pos.astype(F32)[:, None] * inv[None, :]
    cos, sin = jnp.cos(ang), jnp.sin(ang)
    return jnp.concatenate([cos, cos], axis=-1), jnp.concatenate([-sin, sin], axis=-1)


def _hybrid_layer(x, pos0, attend, s_ret0, conv_buf, w):
    b, t, _ = x.shape
    n = b * t
    x2 = x.reshape(n, D_MODEL)
    cos_t, sin_t = _rotation_tables(pos0 + jnp.arange(t))
    tm = min(ROW_TILE, n)
    if t < tm:
        cos_t, sin_t = jnp.tile(cos_t, (tm // t, 1)), jnp.tile(sin_t, (tm // t, 1))
    q, k, v, kb, vb, qr, kr, vr, gate = _proj(x2, w["w_in"], cos_t, sin_t)
    three = lambda a: a.reshape(b, t, a.shape[-1])
    heads = lambda a: a.reshape(b, t, SB_HEADS, SB_HD)
    k, v = heads(k), heads(v)
    o_sb = attend(three(q), three(kb), three(vb), w["sb_bias"])
    o_ret, s_new = _retention(three(qr), three(kr), three(vr), three(gate), s_ret0)
    h = _mix(o_sb.reshape(n, SB_WIDTH), o_ret.reshape(n, RET_WIDTH), x2,
             w["w_o"], w["ln1_g"], w["ln1_b"])
    y, conv_new = _ffn(three(h), conv_buf, w["w_up"], w["conv_w"], w["conv_b"], w["w_down"],
                       w["ln2_g"], w["ln2_b"])
    return y, k, v, s_new, conv_new


def kernel(x_prompt, x_sample, cache_k, cache_v, page_table, state_ret, state_conv, w_in, sb_bias,
           w_o, ln1_g, ln1_b, w_up, conv_w, conv_b, w_down, ln2_g, ln2_b):
    xp, xs = x_prompt, x_sample
    outs = [[] for _ in range(8)]
    for l in range(DEPTH):
        w = dict(w_in=w_in[l].astype(BF16), sb_bias=sb_bias[l], w_o=w_o[l].astype(BF16),
                 ln1_g=ln1_g[l][None], ln1_b=ln1_b[l][None], w_up=w_up[l].astype(BF16),
                 conv_w=conv_w[l], conv_b=conv_b[l][None], w_down=w_down[l].astype(BF16),
                 ln2_g=ln2_g[l][None], ln2_b=ln2_b[l][None])
        bp = xp.shape[0]
        s0 = jnp.zeros((bp, RET_HEADS, RET_DK, RET_DK), F32)
        c0 = jnp.zeros((bp, CONV_W - 1, 2 * D_FF), F32)
        xp, kp, vp, sp, cp = _hybrid_layer(xp, 0, _sb_prompt, s0, c0, w)
        past_len = page_table.shape[1] * cache_k.shape[2]
        attend = functools.partial(_sb_decode, cache_k=cache_k[l], cache_v=cache_v[l],
                                   page_table=page_table)
        xs, kk, vv, ss, cs = _hybrid_layer(xs, past_len, attend, state_ret[l], state_conv[l], w)
        for lst, a in zip(outs, (kp, vp, sp, cp, kk, vv, ss, cs)):
            lst.append(a)
    return (xp, xs) + tuple(jnp.stack(lst) for lst in outs)
```

```python
import functools

import jax
import jax.numpy as jnp
from jax import lax
from jax.experimental import pallas as pl
from jax.experimental.pallas import tpu as pltpu

F32 = jnp.float32
BF16 = jnp.bfloat16

D_MODEL = 1024
DEPTH = 1
SB_HD = 64
SB_WIDTH = D_MODEL // 2
SB_HEADS = SB_WIDTH // SB_HD
SB_SCALE = SB_HD ** -0.5
RET_HEADS = 4
RET_WIDTH = D_MODEL - SB_WIDTH
RET_DK = RET_WIDTH // RET_HEADS
RET_CHUNK = 128
ROT_BASE = 10000.0
PROJ_WIDTH = 3 * SB_WIDTH + 4 * RET_WIDTH
D_FF = ((8 * D_MODEL // 3 + 127) // 128) * 128
CONV_W = 3
ALPHA = (2.0 * DEPTH) ** 0.25
LN_EPS = 1e-5
LOG2E = 1.4426950408889634
MASK_BIAS = -1e30

LANES = 128
SUBLANES = 8
VMEM_LIMIT = 56 * 1024 * 1024

ROW_TILE = 512
FFN_ROW_TILE = 512
FFN_CHUNKS = 2
SB_KEY_BLOCK = 256
SB_QUERY_TILE = 512
DEC_PAGES_PER_STEP = 16


def _params(*sem):
    return pltpu.CompilerParams(dimension_semantics=sem, vmem_limit_bytes=VMEM_LIMIT)


def _resident(shape):
    nd = len(shape)
    return pl.BlockSpec(shape, lambda *_: (0,) * nd, pipeline_mode=pl.Buffered(1))


def _layer_norm(x, g, b):
    mu = jnp.mean(x, axis=-1, keepdims=True)
    d = x - mu
    var = jnp.mean(d * d, axis=-1, keepdims=True)
    return d * lax.rsqrt(var + LN_EPS) * g + b


def _proj_kernel(x_ref, w_ref, cos_ref, sin_ref,
                 q_o, k_o, v_o, kb_o, vb_o, qr_o, kr_o, vr_o, g_o):
    xb = x_ref[...].astype(BF16)

    def mm(c0, width):
        return jnp.dot(xb, w_ref[:, c0:c0 + width], preferred_element_type=F32)

    q_o[...] = (mm(0, SB_WIDTH) * (SB_SCALE * LOG2E)).astype(BF16)
    k = mm(SB_WIDTH, SB_WIDTH)
    k_o[...] = k
    kb_o[...] = k.astype(BF16)
    v = mm(2 * SB_WIDTH, SB_WIDTH)
    v_o[...] = v
    vb_o[...] = v.astype(BF16)

    cos = cos_ref[...]
    sin = sin_ref[...]

    def rotate_into(out_ref, xr, scale):
        for h in range(RET_HEADS):
            xh = xr[:, h * RET_DK:(h + 1) * RET_DK]
            r = xh * cos + pltpu.roll(xh, RET_DK // 2, axis=1) * sin
            if scale is not None:
                r = r * scale
            out_ref[:, h * RET_DK:(h + 1) * RET_DK] = r.astype(BF16)

    base = 3 * SB_WIDTH
    rotate_into(qr_o, mm(base, RET_WIDTH), None)
    rotate_into(kr_o, mm(base + RET_WIDTH, RET_WIDTH), RET_DK ** -0.5)
    vr_o[...] = mm(base + 2 * RET_WIDTH, RET_WIDTH).astype(BF16)
    g_o[...] = mm(base + 3 * RET_WIDTH, RET_WIDTH)


def _proj(x2, w_in_bf, cos_t, sin_t):
    n = x2.shape[0]
    tm = min(ROW_TILE, n)
    n_tab = cos_t.shape[0] // tm
    row = lambda i: (i, 0)
    tab = lambda i: (i % n_tab, 0)
    half = lambda dt: jax.ShapeDtypeStruct((n, SB_WIDTH), dt)
    spec = pl.BlockSpec((tm, SB_WIDTH), row)
    return pl.pallas_call(
        _proj_kernel,
        grid=(n // tm,),
        in_specs=[pl.BlockSpec((tm, D_MODEL), row), _resident(w_in_bf.shape),
                  pl.BlockSpec((tm, RET_DK), tab), pl.BlockSpec((tm, RET_DK), tab)],
        out_specs=[spec] * 9,
        out_shape=[half(BF16), half(F32), half(F32), half(BF16), half(BF16),
                   half(BF16), half(BF16), half(BF16), half(F32)],
        compiler_params=_params("parallel"),
        name="proj",
    )(x2, w_in_bf, cos_t, sin_t)


def _neg_abs(x):
    bits = lax.bitcast_convert_type(x, jnp.uint32) | jnp.uint32(0x80000000)
    return lax.bitcast_convert_type(bits, F32)


def _log2_sigmoids(y):
    ls = jnp.minimum(y, 0.0) - jnp.log2(1.0 + jnp.exp2(_neg_abs(y)))
    return ls, ls - y


def _suffix_matrix(n):
    j = lax.broadcasted_iota(jnp.int32, (n, n), 0)
    s = lax.broadcasted_iota(jnp.int32, (n, n), 1)
    return (j > s).astype(BF16)


def _sb_prompt_kernel(bias_ref, q_ref, k_ref, v_ref, u_ref, o_ref, mb_ref, acc_ref, carry_ref,
                      s0, s1, ls0, ls1, hi0, hi1, tot0, tot1, w0, w1, *, tq):
    pair = pl.program_id(1)
    tk = u_ref.shape[0]
    ratio = tq // tk
    nq = q_ref.shape[0] // tq
    n_items = ratio * nq * (nq + 1) // 2
    n_lanes = tk // LANES
    lane = lax.broadcasted_iota(jnp.int32, (tq, LANES), 1)
    low = lane < SB_HD
    s_buf, ls_buf, hi_buf, tot_buf, w_buf = (s0, s1), (ls0, ls1), (hi0, hi1), (tot0, tot1), (w0, w1)

    row = lax.broadcasted_iota(jnp.int32, (tq, tk), 0)
    col = lax.broadcasted_iota(jnp.int32, (tq, tk), 1)
    for h in range(2):
        b2 = bias_ref[2 * pair + h] * LOG2E
        for j in range(ratio):
            causal = col + (ratio - 1 - j) * tk < row
            mb_ref[h, j] = jnp.where(causal, 0.0, MASK_BIAS) + b2
        mb_ref[h, ratio] = jnp.zeros((tq, tk), F32) + b2
    for ref in (acc_ref, carry_ref) + s_buf + ls_buf + hi_buf + tot_buf + w_buf:
        ref[...] = jnp.zeros_like(ref)

    def key_block(qi, j):
        return pl.ds(pl.multiple_of(((qi + 1) * ratio - 1 - j) * tk, tk), tk)

    def step(t, items, p):
        (qi0, j0), (_, j1), (_, j2), (qi3, j3) = items
        wr, rd = p, 1 - p

        def scores():
            q = q_ref[pl.ds(pl.multiple_of(qi0 * tq, tq), tq), :]
            k = k_ref[key_block(qi0, j0), :]
            zero = jnp.zeros_like(q)
            for h, qh in enumerate((jnp.where(low, q, zero), jnp.where(low, zero, q))):
                s_buf[wr][h] = lax.dot_general(qh, k, (((1,), (1,)), ((), ())),
                                               preferred_element_type=F32)

        def logits():
            variant = jnp.minimum(j1, ratio)
            for h in range(2):
                ls, lstay = _log2_sigmoids(s_buf[rd][h] + mb_ref[h, variant])
                ls_buf[wr][h] = ls
                hi_buf[wr][h] = lstay.astype(BF16)
                tot_buf[wr][h] = jnp.broadcast_to(jnp.sum(lstay, axis=1, keepdims=True),
                                                  (tq, LANES))

        def weights():
            for h in range(2):
                suf = jnp.dot(hi_buf[rd][h], u_ref[...], preferred_element_type=F32)
                c = carry_ref[h]
                cc = jnp.concatenate([c] * n_lanes, axis=1)
                w_buf[wr][h] = jnp.exp2(ls_buf[rd][h] + suf + cc).astype(BF16)
                carry_ref[h] = c + tot_buf[rd][h]

        def outputs():
            v = v_ref[key_block(qi3, j3), :]
            for h in range(2):
                acc_ref[h] += jnp.dot(w_buf[rd][h], v, preferred_element_type=F32)

        outputs()
        scores()
        logits()
        weights()

        @pl.when((t >= 3) & (t - 3 < n_items) & (j3 == (qi3 + 1) * ratio - 1))
        def _():
            o_ref[pl.ds(pl.multiple_of(qi3 * tq, tq), tq), :] = (
                jnp.where(low, acc_ref[0], acc_ref[1]).astype(BF16))

        @pl.when(j2 == 0)
        def _():
            acc_ref[...] = jnp.zeros_like(acc_ref)

        @pl.when(j1 == 0)
        def _():
            carry_ref[...] = jnp.zeros_like(carry_ref)

        wrap = j0 == (qi0 + 1) * ratio - 1
        done = wrap & (qi0 == nq - 1)
        nxt = (jnp.where(wrap & ~done, qi0 + 1, qi0), jnp.where(done, j0, jnp.where(wrap, 0, j0 + 1)))
        return (nxt,) + items[:3]

    def body(i, items):
        return step(2 * i + 1, step(2 * i, items, 0), 1)

    zero_item = (jnp.int32(0), jnp.int32(0))
    lax.fori_loop(0, (n_items + 3 + 1) // 2, body, (zero_item,) * 4)


def _sb_prompt(q3, k3, v3, bias):
    b, t, _ = q3.shape
    tk = SB_KEY_BLOCK
    tq = SB_QUERY_TILE if t % SB_QUERY_TILE == 0 else tk
    n_pairs = SB_WIDTH // LANES
    u = _suffix_matrix(tk)
    spec = pl.BlockSpec((None, t, LANES), lambda bi, p: (bi, 0, p))
    f32 = lambda n: pltpu.VMEM((2, tq, n), F32)
    bf16 = lambda n: pltpu.VMEM((2, tq, n), BF16)
    return pl.pallas_call(
        functools.partial(_sb_prompt_kernel, tq=tq),
        grid=(b, n_pairs),
        in_specs=[pl.BlockSpec(memory_space=pltpu.SMEM), spec, spec, spec, _resident(u.shape)],
        out_specs=spec,
        out_shape=jax.ShapeDtypeStruct(q3.shape, BF16),
        scratch_shapes=[pltpu.VMEM((2, tq // tk + 1, tq, tk), F32), f32(LANES), f32(LANES),
                        f32(tk), f32(tk), f32(tk), f32(tk), bf16(tk), bf16(tk),
                        f32(LANES), f32(LANES), bf16(tk), bf16(tk)],
        compiler_params=_params("parallel", "parallel"),
        name="sb_prompt",
    )(bias, q3, k3, v3, u)


def _sb_decode_kernel(pt_ref, bias_ref, q_ref, kn_ref, vn_ref, u_ref, *rest, n_pages_step):
    del pt_ref
    g = n_pages_step
    k_refs, v_refs = rest[:g], rest[g:2 * g]
    o_ref, qbd_ref, biasv_ref, acc_ref, carry_ref = rest[2 * g:]
    step = pl.program_id(1)
    t_new = q_ref.shape[0]
    rows, page = biasv_ref.shape
    contract_last = (((1,), (1,)), ((), ()))

    def weights(y, mask):
        n_blk = y.shape[1] // page
        logs = []
        for i in range(n_blk):
            yi = y[:, i * page:(i + 1) * page] + biasv_ref[...]
            if mask is not None:
                yi = jnp.where(mask, yi, MASK_BIAS)
            logs.append(_log2_sigmoids(yi))
        stay = jnp.concatenate([lstay.astype(BF16) for _, lstay in logs], axis=0)
        suf = jnp.dot(stay, u_ref[...], preferred_element_type=F32)
        carry = carry_ref[...]
        ws = [None] * n_blk
        for i in reversed(range(n_blk)):
            ls, lstay = logs[i]
            ws[i] = jnp.exp2(ls + suf[i * rows:(i + 1) * rows] + carry).astype(BF16)
            carry = carry + jnp.sum(lstay, axis=1, keepdims=True)
        carry_ref[...] = carry
        return jnp.concatenate(ws, axis=1)

    @pl.when(step == 0)
    def _init():
        q = q_ref[...].astype(F32)
        q_rep = jnp.concatenate([q] * SB_HEADS, axis=0)
        r = lax.broadcasted_iota(jnp.int32, (rows, SB_WIDTH), 0) // t_new
        c = lax.broadcasted_iota(jnp.int32, (rows, SB_WIDTH), 1) // SB_HD
        qbd_ref[...] = jnp.where(r == c, q_rep, 0.0).astype(BF16)
        rh = lax.broadcasted_iota(jnp.int32, (rows, page), 0) // t_new
        bv = jnp.zeros((rows, page), F32)
        for h in range(SB_HEADS):
            bv = jnp.where(rh == h, bias_ref[h] * LOG2E, bv)
        biasv_ref[...] = bv
        acc_ref[...] = jnp.zeros_like(acc_ref)
        carry_ref[...] = jnp.zeros_like(carry_ref)
        pad = jnp.zeros((page - t_new, SB_WIDTH), F32)
        kn = jnp.concatenate([kn_ref[...].astype(F32), pad], axis=0).astype(BF16)
        vn = jnp.concatenate([vn_ref[...].astype(F32), pad], axis=0).astype(BF16)
        i = lax.broadcasted_iota(jnp.int32, (rows, page), 0) % t_new
        j = lax.broadcasted_iota(jnp.int32, (rows, page), 1)
        y = lax.dot_general(qbd_ref[...], kn, contract_last, preferred_element_type=F32)
        acc_ref[...] += jnp.dot(weights(y, j < i), vn, preferred_element_type=F32)

    k_t = jnp.concatenate([r[...] for r in k_refs], axis=1).astype(BF16)
    y = jnp.dot(qbd_ref[...], k_t, preferred_element_type=F32)
    w = weights(y, None)
    v_t = jnp.concatenate([r[...] for r in v_refs], axis=1).astype(BF16)
    acc_ref[...] += lax.dot_general(w, v_t, contract_last, preferred_element_type=F32)

    @pl.when(step == pl.num_programs(1) - 1)
    def _finish():
        lane = lax.broadcasted_iota(jnp.int32, (t_new, LANES), 1)
        for p in range(SB_WIDTH // LANES):
            cols = slice(p * LANES, (p + 1) * LANES)
            even = acc_ref[2 * p * t_new:(2 * p + 1) * t_new, cols]
            odd = acc_ref[(2 * p + 1) * t_new:(2 * p + 2) * t_new, cols]
            o_ref[:, cols] = jnp.where(lane < SB_HD, even, odd).astype(BF16)


def _sb_decode(q3, kn3, vn3, bias, cache_k, cache_v, page_table):
    b, t_new, _ = q3.shape
    n_pages = page_table.shape[1]
    n_phys, page_size = cache_k.shape[:2]
    assert page_size == LANES and t_new == SUBLANES
    g = DEC_PAGES_PER_STEP
    while n_pages % g:
        g //= 2
    n_steps = n_pages // g
    ck = jnp.transpose(cache_k, (0, 2, 3, 1)).reshape(n_phys, SB_WIDTH, page_size)
    cv = jnp.transpose(cache_v, (0, 2, 3, 1)).reshape(n_phys, SB_WIDTH, page_size)
    u = _suffix_matrix(page_size)
    rows = SB_HEADS * t_new
    new = pl.BlockSpec((None, t_new, SB_WIDTH), lambda bi, s, pt: (bi, 0, 0))

    def page_spec(i):
        return pl.BlockSpec((None, SB_WIDTH, page_size),
                            lambda bi, s, pt: (pt[bi, (n_steps - 1 - s) * g + i], 0, 0))

    grid_spec = pltpu.PrefetchScalarGridSpec(
        num_scalar_prefetch=1,
        grid=(b, n_steps),
        in_specs=[pl.BlockSpec(memory_space=pltpu.SMEM), new, new, new,
                  pl.BlockSpec(u.shape, lambda bi, s, pt: (0, 0))]
                 + [page_spec(i) for i in range(g)] * 2,
        out_specs=new,
        scratch_shapes=[pltpu.VMEM((rows, SB_WIDTH), BF16), pltpu.VMEM((rows, page_size), F32),
                        pltpu.VMEM((rows, SB_WIDTH), F32), pltpu.VMEM((rows, page_size), F32)],
    )
    return pl.pallas_call(
        functools.partial(_sb_decode_kernel, n_pages_step=g),
        grid_spec=grid_spec,
        out_shape=jax.ShapeDtypeStruct(q3.shape, BF16),
        compiler_params=_params("parallel", "arbitrary"),
        name="sb_decode",
    )(page_table, bias, q3, kn3, vn3, u, *([ck] * g), *([cv] * g))


def _ret_kernel(qr_ref, kr_ref, vr_ref, g_ref, s0_ref, dm_ref, qd_ref, kd_ref, cd_ref,
                o_ref, s_out_ref, s_scr, *, chunk):
    t = pl.program_id(1)
    tt = qr_ref.shape[0]
    pad = tt < chunk
    n_chunks = 1 if pad else tt // chunk

    @pl.when(t == 0)
    def _():
        s_scr[...] = s0_ref[...]

    def load(ref, rows, cols):
        x = ref[rows, cols]
        if pad:
            x = jnp.concatenate([x.astype(F32), jnp.zeros((chunk - tt, x.shape[1]), F32)], axis=0)
        return x

    for h in range(RET_HEADS):
        cols = slice(h * RET_DK, (h + 1) * RET_DK)
        s = s_scr[h]
        for c in range(n_chunks):
            rows = slice(0, tt) if pad else slice(c * chunk, (c + 1) * chunk)
            qc = load(qr_ref, rows, cols).astype(BF16)
            kc = load(kr_ref, rows, cols)
            vc = load(vr_ref, rows, cols).astype(BF16)
            att = lax.dot_general(qc, kc.astype(BF16), (((1,), (1,)), ((), ())),
                                  preferred_element_type=F32) * dm_ref[h]
            o = jnp.dot(att.astype(BF16), vc, preferred_element_type=F32)
            o = o + jnp.dot(qc, s.astype(BF16), preferred_element_type=F32) * qd_ref[h]
            kdec = (kc.astype(F32) * kd_ref[h]).astype(BF16)
            s = s * cd_ref[h] + lax.dot_general(kdec, vc, (((0,), (0,)), ((), ())),
                                                preferred_element_type=F32)
            mu = jnp.mean(o, axis=-1, keepdims=True)
            d = o - mu
            var = jnp.mean(d * d, axis=-1, keepdims=True)
            on = d * lax.rsqrt(var + LN_EPS)
            gate = g_ref[rows, cols]
            swish = gate * (1.0 / (1.0 + jnp.exp(-gate)))
            o_ref[rows, cols] = ((on[:tt] if pad else on) * swish).astype(BF16)
        s_scr[h] = s

    @pl.when(t == pl.num_programs(1) - 1)
    def _():
        s_out_ref[...] = s_scr[...]


def _retention_tables(chunk, true_chunk):
    lg = jnp.log(1.0 - 2.0 ** (-5.0 - jnp.arange(RET_HEADS, dtype=F32)))
    idx = jnp.arange(chunk, dtype=F32)
    diff = idx[:, None] - idx[None, :]
    dm = jnp.where(diff[None] >= 0, jnp.exp(jnp.maximum(diff, 0.0)[None] * lg[:, None, None]), 0.0)
    qd = jnp.exp((idx[None, :] + 1.0) * lg[:, None])
    kd = jnp.exp((true_chunk - 1.0 - idx)[None, :] * lg[:, None])
    kd = jnp.where(idx[None, :] < true_chunk, kd, 0.0)
    cd = jnp.exp(true_chunk * lg)
    rep = lambda a: jnp.broadcast_to(a[:, :, None], (RET_HEADS, chunk, RET_DK))
    cd = jnp.broadcast_to(cd[:, None, None], (RET_HEADS, SUBLANES, RET_DK))
    return dm, rep(qd), rep(kd), cd[:, :1, :]


def _retention(qr3, kr3, vr3, g3, s0):
    b, t, _ = qr3.shape
    chunk = RET_CHUNK
    true_chunk = chunk if t % chunk == 0 else t
    assert true_chunk <= chunk
    tt = min(ROW_TILE, t)
    dm, qd, kd, cd = _retention_tables(chunk, true_chunk)
    seq = pl.BlockSpec((None, tt, RET_WIDTH), lambda bi, ti: (bi, ti, 0))
    state = pl.BlockSpec((None,) + s0.shape[1:], lambda bi, ti: (bi, 0, 0, 0))
    return pl.pallas_call(
        functools.partial(_ret_kernel, chunk=chunk),
        grid=(b, t // tt),
        in_specs=[seq, seq, seq, seq, state] + [_resident(a.shape) for a in (dm, qd, kd, cd)],
        out_specs=[seq, state],
        out_shape=[jax.ShapeDtypeStruct(qr3.shape, BF16), jax.ShapeDtypeStruct(s0.shape, F32)],
        scratch_shapes=[pltpu.VMEM(s0.shape[1:], F32)],
        compiler_params=_params("parallel", "arbitrary"),
        name="retention",
    )(qr3, kr3, vr3, g3, s0, dm, qd, kd, cd)


def _mix_kernel(osb_ref, oret_ref, x_ref, wo_ref, g_ref, b_ref, h_ref):
    mix = jnp.dot(osb_ref[...], wo_ref[:SB_WIDTH, :], preferred_element_type=F32)
    mix = mix + jnp.dot(oret_ref[...], wo_ref[SB_WIDTH:, :], preferred_element_type=F32)
    h_ref[...] = _layer_norm(ALPHA * x_ref[...] + mix, g_ref[...], b_ref[...])


def _mix(o_sb, o_ret, x2, w_o_bf, ln_g, ln_b):
    n = x2.shape[0]
    tm = min(ROW_TILE, n)
    row = lambda i: (i, 0)
    return pl.pallas_call(
        _mix_kernel,
        grid=(n // tm,),
        in_specs=[pl.BlockSpec((tm, SB_WIDTH), row), pl.BlockSpec((tm, RET_WIDTH), row),
                  pl.BlockSpec((tm, D_MODEL), row), _resident(w_o_bf.shape),
                  _resident(ln_g.shape), _resident(ln_b.shape)],
        out_specs=pl.BlockSpec((tm, D_MODEL), row),
        out_shape=jax.ShapeDtypeStruct(x2.shape, F32),
        compiler_params=_params("parallel"),
        name="mix_ln1",
    )(o_sb, o_ret, x2, w_o_bf, ln_g, ln_b)


def _gelu_tanh(x):
    return x * (0.5 * (1.0 + jnp.tanh(0.7978845608028654 * (x + 0.044715 * (x * x * x)))))


def _ffn_kernel(h_ref, prev_ref, wup_ref, cw_ref, cb_ref, wdn_ref, g_ref, b_ref,
                y_ref, cs_ref, carry_scr, *, n_seq, use_carry):
    rows = h_ref.shape[0]
    tt = rows // n_seq
    if use_carry:
        @pl.when(pl.program_id(1) == 0)
        def _():
            carry_scr[...] = prev_ref[...]
        prev = carry_scr
    else:
        prev = prev_ref

    h = h_ref[...]
    hb = h.astype(BF16)
    cw = D_FF // FFN_CHUNKS
    tpos = lax.broadcasted_iota(jnp.int32, (n_seq, tt, cw), 1)

    def conv_cols(c0):
        u = jnp.dot(hb, wup_ref[:, c0:c0 + cw], preferred_element_type=F32)
        u3 = u.reshape(n_seq, tt, cw)
        p0 = prev[:, 0:1, c0:c0 + cw]
        p1 = prev[:, 1:2, c0:c0 + cw]
        r1 = pltpu.roll(u, 1, axis=0).reshape(n_seq, tt, cw)
        r2 = pltpu.roll(u, 2, axis=0).reshape(n_seq, tt, cw)
        um1 = jnp.where(tpos == 0, p1, r1)
        um2 = jnp.where(tpos == 0, p0, jnp.where(tpos == 1, p1, r2))
        w = cw_ref[:, c0:c0 + cw]
        c = cb_ref[:, c0:c0 + cw] + w[0:1] * um2 + w[1:2] * um1 + w[2:3] * u3
        last = u3[:, tt - 2:tt, :]
        cs_ref[:, :, c0:c0 + cw] = last
        if use_carry:
            carry_scr[:, :, c0:c0 + cw] = last
        return c.reshape(rows, cw)

    f = None
    for j in range(FFN_CHUNKS):
        ca = conv_cols(j * cw)
        cb = conv_cols(D_FF + j * cw)
        gated = (_gelu_tanh(ca) * cb).astype(BF16)
        part = jnp.dot(gated, wdn_ref[j * cw:(j + 1) * cw, :], preferred_element_type=F32)
        f = part if f is None else f + part
    y_ref[...] = _layer_norm(ALPHA * h + f, g_ref[...], b_ref[...])


def _ffn(h3, conv0, w_up_bf, conv_w, conv_b, w_down_bf, ln_g, ln_b):
    b, t, _ = h3.shape
    weights = [_resident(a.shape) for a in (w_up_bf, conv_w, conv_b, w_down_bf, ln_g, ln_b)]
    if t >= FFN_ROW_TILE:
        tm, n_seq = FFN_ROW_TILE, 1
        h_in = h3
        grid = (b, t // tm)
        hspec = pl.BlockSpec((None, tm, D_MODEL), lambda bi, ti: (bi, ti, 0))
        cspec = pl.BlockSpec((1, CONV_W - 1, 2 * D_FF), lambda bi, ti: (bi, 0, 0))
        y_shape = h3.shape
    else:
        n_seq = FFN_ROW_TILE // t
        while b % n_seq:
            n_seq //= 2
        tm = n_seq * t
        h_in = h3.reshape(b * t, D_MODEL)
        grid = (b // n_seq, 1)
        hspec = pl.BlockSpec((tm, D_MODEL), lambda bi, ti: (bi, 0))
        cspec = pl.BlockSpec((n_seq, CONV_W - 1, 2 * D_FF), lambda bi, ti: (bi, 0, 0))
        y_shape = h_in.shape
    y, cs = pl.pallas_call(
        functools.partial(_ffn_kernel, n_seq=n_seq, use_carry=t >= FFN_ROW_TILE),
        grid=grid,
        in_specs=[hspec, cspec] + weights,
        out_specs=[hspec, cspec],
        out_shape=[jax.ShapeDtypeStruct(y_shape, F32), jax.ShapeDtypeStruct(conv0.shape, F32)],
        scratch_shapes=[pltpu.VMEM((n_seq, CONV_W - 1, 2 * D_FF), F32)],
        compiler_params=_params("parallel", "arbitrary"),
        name="conv_ffn",
    )(h_in, conv0, w_up_bf, conv_w, conv_b, w_down_bf, ln_g, ln_b)
    return y.reshape(h3.shape), cs


def _rotation_tables(pos):
    half = RET_DK // 2
    inv = 1.0 / (ROT_BASE ** jnp.linspace(0.0, 1.0, half, dtype=F32))
    ang = pos.astype(F32)[:, None] * inv[None, :]
    cos, sin = jnp.cos(ang), jnp.sin(ang)
    return jnp.concatenate([cos, cos], axis=-1), jnp.concatenate([-sin, sin], axis=-1)


def _hybrid_layer(x, pos0, attend, s_ret0, conv_buf, w):
    b, t, _ = x.shape
    n = b * t
    x2 = x.reshape(n, D_MODEL)
    cos_t, sin_t = _rotation_tables(pos0 + jnp.arange(t))
    tm = min(ROW_TILE, n)
    if t < tm:
        cos_t, sin_t = jnp.tile(cos_t, (tm // t, 1)), jnp.tile(sin_t, (tm // t, 1))
    q, k, v, kb, vb, qr, kr, vr, gate = _proj(x2, w["w_in"], cos_t, sin_t)
    three = lambda a: a.reshape(b, t, a.shape[-1])
    heads = lambda a: a.reshape(b, t, SB_HEADS, SB_HD)
    k, v = heads(k), heads(v)
    o_sb = attend(three(q), three(kb), three(vb), w["sb_bias"])
    o_ret, s_new = _retention(three(qr), three(kr), three(vr), three(gate), s_ret0)
    h = _mix(o_sb.reshape(n, SB_WIDTH), o_ret.reshape(n, RET_WIDTH), x2,
             w["w_o"], w["ln1_g"], w["ln1_b"])
    y, conv_new = _ffn(three(h), conv_buf, w["w_up"], w["conv_w"], w["conv_b"], w["w_down"],
                       w["ln2_g"], w["ln2_b"])
    return y, k, v, s_new, conv_new


def kernel(x_prompt, x_sample, cache_k, cache_v, page_table, state_ret, state_conv, w_in, sb_bias,
           w_o, ln1_g, ln1_b, w_up, conv_w, conv_b, w_down, ln2_g, ln2_b):
    xp, xs = x_prompt, x_sample
    outs = [[] for _ in range(8)]
    for l in range(DEPTH):
        w = dict(w_in=w_in[l].astype(BF16), sb_bias=sb_bias[l], w_o=w_o[l].astype(BF16),
                 ln1_g=ln1_g[l][None], ln1_b=ln1_b[l][None], w_up=w_up[l].astype(BF16),
                 conv_w=conv_w[l], conv_b=conv_b[l][None], w_down=w_down[l].astype(BF16),
                 ln2_g=ln2_g[l][None], ln2_b=ln2_b[l][None])
        bp = xp.shape[0]
        s0 = jnp.zeros((bp, RET_HEADS, RET_DK, RET_DK), F32)
        c0 = jnp.zeros((bp, CONV_W - 1, 2 * D_FF), F32)
        xp, kp, vp, sp, cp = _hybrid_layer(xp, 0, _sb_prompt, s0, c0, w)
        past_len = page_table.shape[1] * cache_k.shape[2]
        attend = functools.partial(_sb_decode, cache_k=cache_k[l], cache_v=cache_v[l],
                                   page_table=page_table)
        xs, kk, vv, ss, cs = _hybrid_layer(xs, past_len, attend, state_ret[l], state_conv[l], w)
        for lst, a in zip(outs, (kp, vp, sp, cp, kk, vv, ss, cs)):
            lst.append(a)
    return (xp, xs) + tuple(jnp.stack(lst) for lst in outs)
```

```python
import functools

import jax
import jax.numpy as jnp
from jax import lax
from jax.experimental import pallas as pl
from jax.experimental.pallas import tpu as pltpu

F32 = jnp.float32
BF16 = jnp.bfloat16

D_MODEL = 1024
DEPTH = 1
SB_HD = 64
SB_WIDTH = D_MODEL // 2
SB_HEADS = SB_WIDTH // SB_HD
SB_SCALE = SB_HD ** -0.5
RET_HEADS = 4
RET_WIDTH = D_MODEL - SB_WIDTH
RET_DK = RET_WIDTH // RET_HEADS
RET_CHUNK = 128
ROT_BASE = 10000.0
PROJ_WIDTH = 3 * SB_WIDTH + 4 * RET_WIDTH
D_FF = ((8 * D_MODEL // 3 + 127) // 128) * 128
CONV_W = 3
ALPHA = (2.0 * DEPTH) ** 0.25
LN_EPS = 1e-5
LOG2E = 1.4426950408889634
MASK_BIAS = -1e30

LANES = 128
SUBLANES = 8
VMEM_LIMIT = 56 * 1024 * 1024

ROW_TILE = 512
FFN_ROW_TILE = 512
FFN_CHUNKS = 2
SB_KEY_BLOCK = 256
SB_QUERY_TILE = 1024
DEC_PAGES_PER_STEP = 32


def _params(*sem):
    return pltpu.CompilerParams(dimension_semantics=sem, vmem_limit_bytes=VMEM_LIMIT)


def _resident(shape):
    nd = len(shape)
    return pl.BlockSpec(shape, lambda *_: (0,) * nd, pipeline_mode=pl.Buffered(1))


def _layer_norm(x, g, b):
    mu = jnp.mean(x, axis=-1, keepdims=True)
    d = x - mu
    var = jnp.mean(d * d, axis=-1, keepdims=True)
    return d * lax.rsqrt(var + LN_EPS) * g + b


def _proj_kernel(x_ref, w_ref, cos_ref, sin_ref,
                 q_o, k_o, v_o, kb_o, vb_o, qr_o, kr_o, vr_o, g_o):
    xb = x_ref[...].astype(BF16)

    def mm(c0, width):
        return jnp.dot(xb, w_ref[:, c0:c0 + width], preferred_element_type=F32)

    q_o[...] = (mm(0, SB_WIDTH) * (SB_SCALE * LOG2E)).astype(BF16)
    def store_heads(out_ref, a):
        for h in range(SB_HEADS):
            out_ref[pl.ds(h, a.shape[0], stride=SB_HEADS), :] = a[:, h * SB_HD:(h + 1) * SB_HD]

    k = mm(SB_WIDTH, SB_WIDTH)
    store_heads(k_o, k)
    kb_o[...] = k.astype(BF16)
    v = mm(2 * SB_WIDTH, SB_WIDTH)
    store_heads(v_o, v)
    vb_o[...] = v.astype(BF16)

    cos = cos_ref[...]
    sin = sin_ref[...]

    def rotate_into(out_ref, xr, scale):
        for h in range(RET_HEADS):
            xh = xr[:, h * RET_DK:(h + 1) * RET_DK]
            r = xh * cos + pltpu.roll(xh, RET_DK // 2, axis=1) * sin
            if scale is not None:
                r = r * scale
            out_ref[:, h * RET_DK:(h + 1) * RET_DK] = r.astype(BF16)

    base = 3 * SB_WIDTH
    rotate_into(qr_o, mm(base, RET_WIDTH), None)
    rotate_into(kr_o, mm(base + RET_WIDTH, RET_WIDTH), RET_DK ** -0.5)
    vr_o[...] = mm(base + 2 * RET_WIDTH, RET_WIDTH).astype(BF16)
    g_o[...] = mm(base + 3 * RET_WIDTH, RET_WIDTH)


def _proj(x2, w_in_bf, cos_t, sin_t):
    n = x2.shape[0]
    tm = min(ROW_TILE, n)
    n_tab = cos_t.shape[0] // tm
    row = lambda i: (i, 0)
    tab = lambda i: (i % n_tab, 0)
    half = lambda dt: jax.ShapeDtypeStruct((n, SB_WIDTH), dt)
    spec = pl.BlockSpec((tm, SB_WIDTH), row)
    by_head = jax.ShapeDtypeStruct((n * SB_HEADS, SB_HD), F32)
    by_head_spec = pl.BlockSpec((tm * SB_HEADS, SB_HD), row)
    return pl.pallas_call(
        _proj_kernel,
        grid=(n // tm,),
        in_specs=[pl.BlockSpec((tm, D_MODEL), row), _resident(w_in_bf.shape),
                  pl.BlockSpec((tm, RET_DK), tab), pl.BlockSpec((tm, RET_DK), tab)],
        out_specs=[spec, by_head_spec, by_head_spec] + [spec] * 6,
        out_shape=[half(BF16), by_head, by_head, half(BF16), half(BF16),
                   half(BF16), half(BF16), half(BF16), half(F32)],
        compiler_params=_params("parallel"),
        name="proj",
    )(x2, w_in_bf, cos_t, sin_t)


def _neg_abs(x):
    bits = lax.bitcast_convert_type(x, jnp.uint32) | jnp.uint32(0x80000000)
    return lax.bitcast_convert_type(bits, F32)


def _log2_sigmoids(y):
    ls = jnp.minimum(y, 0.0) - jnp.log2(1.0 + jnp.exp2(_neg_abs(y)))
    return ls, ls - y


def _suffix_matrix(n):
    j = lax.broadcasted_iota(jnp.int32, (n, n), 0)
    s = lax.broadcasted_iota(jnp.int32, (n, n), 1)
    return (j > s).astype(BF16)


def _sb_prompt_kernel(bias_ref, q_ref, k_ref, v_ref, u_ref, o_ref, mb_ref, acc_ref, carry_ref,
                      s0, s1, ls0, ls1, hi0, hi1, tot0, tot1, w0, w1, *, tq):
    pair = pl.program_id(1)
    tk = u_ref.shape[0]
    ratio = tq // tk
    nq = q_ref.shape[0] // tq
    n_items = ratio * nq * (nq + 1) // 2
    n_lanes = tk // LANES
    lane = lax.broadcasted_iota(jnp.int32, (tq, LANES), 1)
    low = lane < SB_HD
    s_buf, ls_buf, hi_buf, tot_buf, w_buf = (s0, s1), (ls0, ls1), (hi0, hi1), (tot0, tot1), (w0, w1)

    row = lax.broadcasted_iota(jnp.int32, (tq, tk), 0)
    col = lax.broadcasted_iota(jnp.int32, (tq, tk), 1)
    for h in range(2):
        b2 = bias_ref[2 * pair + h] * LOG2E
        for j in range(ratio):
            causal = col + (ratio - 1 - j) * tk < row
            mb_ref[h, j] = jnp.where(causal, 0.0, MASK_BIAS) + b2
        mb_ref[h, ratio] = jnp.zeros((tq, tk), F32) + b2
    for ref in (acc_ref, carry_ref) + s_buf + ls_buf + hi_buf + tot_buf + w_buf:
        ref[...] = jnp.zeros_like(ref)

    def key_block(qi, j):
        return pl.ds(pl.multiple_of(((qi + 1) * ratio - 1 - j) * tk, tk), tk)

    def step(t, items, p):
        (qi0, j0), (_, j1), (_, j2), (qi3, j3) = items
        wr, rd = p, 1 - p

        def scores():
            q = q_ref[pl.ds(pl.multiple_of(qi0 * tq, tq), tq), :]
            k = k_ref[key_block(qi0, j0), :]
            zero = jnp.zeros_like(q)
            for h, qh in enumerate((jnp.where(low, q, zero), jnp.where(low, zero, q))):
                s_buf[wr][h] = lax.dot_general(qh, k, (((1,), (1,)), ((), ())),
                                               preferred_element_type=F32)

        def logits():
            variant = jnp.minimum(j1, ratio)
            for h in range(2):
                ls, lstay = _log2_sigmoids(s_buf[rd][h] + mb_ref[h, variant])
                ls_buf[wr][h] = ls
                hi_buf[wr][h] = lstay.astype(BF16)
                tot_buf[wr][h] = jnp.broadcast_to(jnp.sum(lstay, axis=1, keepdims=True),
                                                  (tq, LANES))

        def weights():
            for h in range(2):
                suf = jnp.dot(hi_buf[rd][h], u_ref[...], preferred_element_type=F32)
                c = carry_ref[h]
                cc = jnp.concatenate([c] * n_lanes, axis=1)
                w_buf[wr][h] = jnp.exp2(ls_buf[rd][h] + suf + cc).astype(BF16)
                carry_ref[h] = c + tot_buf[rd][h]

        def outputs():
            v = v_ref[key_block(qi3, j3), :]
            for h in range(2):
                acc_ref[h] += jnp.dot(w_buf[rd][h], v, preferred_element_type=F32)

        outputs()
        scores()
        logits()
        weights()

        @pl.when((t >= 3) & (t - 3 < n_items) & (j3 == (qi3 + 1) * ratio - 1))
        def _():
            o_ref[pl.ds(pl.multiple_of(qi3 * tq, tq), tq), :] = (
                jnp.where(low, acc_ref[0], acc_ref[1]).astype(BF16))

        @pl.when(j2 == 0)
        def _():
            acc_ref[...] = jnp.zeros_like(acc_ref)

        @pl.when(j1 == 0)
        def _():
            carry_ref[...] = jnp.zeros_like(carry_ref)

        wrap = j0 == (qi0 + 1) * ratio - 1
        done = wrap & (qi0 == nq - 1)
        nxt = (jnp.where(wrap & ~done, qi0 + 1, qi0), jnp.where(done, j0, jnp.where(wrap, 0, j0 + 1)))
        return (nxt,) + items[:3]

    def body(i, items):
        return step(2 * i + 1, step(2 * i, items, 0), 1)

    zero_item = (jnp.int32(0), jnp.int32(0))
    lax.fori_loop(0, (n_items + 3 + 1) // 2, body, (zero_item,) * 4)


def _sb_prompt(q3, k3, v3, bias):
    b, t, _ = q3.shape
    tk = SB_KEY_BLOCK
    tq = SB_QUERY_TILE if t % SB_QUERY_TILE == 0 else tk
    n_pairs = SB_WIDTH // LANES
    u = _suffix_matrix(tk)
    spec = pl.BlockSpec((None, t, LANES), lambda bi, p: (bi, 0, p))
    f32 = lambda n: pltpu.VMEM((2, tq, n), F32)
    bf16 = lambda n: pltpu.VMEM((2, tq, n), BF16)
    return pl.pallas_call(
        functools.partial(_sb_prompt_kernel, tq=tq),
        grid=(b, n_pairs),
        in_specs=[pl.BlockSpec(memory_space=pltpu.SMEM), spec, spec, spec, _resident(u.shape)],
        out_specs=spec,
        out_shape=jax.ShapeDtypeStruct(q3.shape, BF16),
        scratch_shapes=[pltpu.VMEM((2, tq // tk + 1, tq, tk), F32), f32(LANES), f32(LANES),
                        f32(tk), f32(tk), f32(tk), f32(tk), bf16(tk), bf16(tk),
                        f32(LANES), f32(LANES), bf16(tk), bf16(tk)],
        compiler_params=_params("parallel", "parallel"),
        name="sb_prompt",
    )(bias, q3, k3, v3, u)


def _sb_decode_kernel(pt_ref, bias_ref, q_ref, kn_ref, vn_ref, u_ref, *rest, n_pages_step):
    del pt_ref
    g = n_pages_step
    k_refs, v_refs = rest[:g], rest[g:2 * g]
    o_ref, qbd_ref, biasv_ref, acc_ref, carry_ref = rest[2 * g:]
    step = pl.program_id(1)
    t_new = q_ref.shape[0]
    rows, page = biasv_ref.shape
    contract_last = (((1,), (1,)), ((), ()))

    def weights(y, mask):
        n_blk = y.shape[1] // page
        logs = []
        for i in range(n_blk):
            yi = y[:, i * page:(i + 1) * page] + biasv_ref[...]
            if mask is not None:
                yi = jnp.where(mask, yi, MASK_BIAS)
            logs.append(_log2_sigmoids(yi))
        stay = jnp.concatenate([lstay.astype(BF16) for _, lstay in logs], axis=0)
        suf = jnp.dot(stay, u_ref[...], preferred_element_type=F32)
        carry = carry_ref[...]
        ws = [None] * n_blk
        for i in reversed(range(n_blk)):
            ls, lstay = logs[i]
            ws[i] = jnp.exp2(ls + suf[i * rows:(i + 1) * rows] + carry).astype(BF16)
            carry = carry + jnp.sum(lstay, axis=1, keepdims=True)
        carry_ref[...] = carry
        return jnp.concatenate(ws, axis=1)

    @pl.when(step == 0)
    def _init():
        q = q_ref[...].astype(F32)
        q_rep = jnp.concatenate([q] * SB_HEADS, axis=0)
        r = lax.broadcasted_iota(jnp.int32, (rows, SB_WIDTH), 0) // t_new
        c = lax.broadcasted_iota(jnp.int32, (rows, SB_WIDTH), 1) // SB_HD
        qbd_ref[...] = jnp.where(r == c, q_rep, 0.0).astype(BF16)
        rh = lax.broadcasted_iota(jnp.int32, (rows, page), 0) // t_new
        bv = jnp.zeros((rows, page), F32)
        for h in range(SB_HEADS):
            bv = jnp.where(rh == h, bias_ref[h] * LOG2E, bv)
        biasv_ref[...] = bv
        acc_ref[...] = jnp.zeros_like(acc_ref)
        carry_ref[...] = jnp.zeros_like(carry_ref)
        pad = jnp.zeros((page - t_new, SB_WIDTH), F32)
        kn = jnp.concatenate([kn_ref[...].astype(F32), pad], axis=0).astype(BF16)
        vn = jnp.concatenate([vn_ref[...].astype(F32), pad], axis=0).astype(BF16)
        i = lax.broadcasted_iota(jnp.int32, (rows, page), 0) % t_new
        j = lax.broadcasted_iota(jnp.int32, (rows, page), 1)
        y = lax.dot_general(qbd_ref[...], kn, contract_last, preferred_element_type=F32)
        acc_ref[...] += jnp.dot(weights(y, j < i), vn, preferred_element_type=F32)

    k_t = jnp.concatenate([r[...] for r in k_refs], axis=1).astype(BF16)
    y = jnp.dot(qbd_ref[...], k_t, preferred_element_type=F32)
    w = weights(y, None)
    v_t = jnp.concatenate([r[...] for r in v_refs], axis=1).astype(BF16)
    acc_ref[...] += lax.dot_general(w, v_t, contract_last, preferred_element_type=F32)

    @pl.when(step == pl.num_programs(1) - 1)
    def _finish():
        lane = lax.broadcasted_iota(jnp.int32, (t_new, LANES), 1)
        for p in range(SB_WIDTH // LANES):
            cols = slice(p * LANES, (p + 1) * LANES)
            even = acc_ref[2 * p * t_new:(2 * p + 1) * t_new, cols]
            odd = acc_ref[(2 * p + 1) * t_new:(2 * p + 2) * t_new, cols]
            o_ref[:, cols] = jnp.where(lane < SB_HD, even, odd).astype(BF16)


def _sb_decode(q3, kn3, vn3, bias, cache_k, cache_v, page_table):
    b, t_new, _ = q3.shape
    n_pages = page_table.shape[1]
    n_phys, page_size = cache_k.shape[:2]
    assert page_size == LANES and t_new == SUBLANES
    g = DEC_PAGES_PER_STEP
    while n_pages % g:
        g //= 2
    n_steps = n_pages // g
    ck = jnp.transpose(cache_k, (0, 2, 3, 1)).reshape(n_phys, SB_WIDTH, page_size)
    cv = jnp.transpose(cache_v, (0, 2, 3, 1)).reshape(n_phys, SB_WIDTH, page_size)
    u = _suffix_matrix(page_size)
    rows = SB_HEADS * t_new
    new = pl.BlockSpec((None, t_new, SB_WIDTH), lambda bi, s, pt: (bi, 0, 0))

    def page_spec(i):
        return pl.BlockSpec((None, SB_WIDTH, page_size),
                            lambda bi, s, pt: (pt[bi, (n_steps - 1 - s) * g + i], 0, 0))

    grid_spec = pltpu.PrefetchScalarGridSpec(
        num_scalar_prefetch=1,
        grid=(b, n_steps),
        in_specs=[pl.BlockSpec(memory_space=pltpu.SMEM), new, new, new,
                  pl.BlockSpec(u.shape, lambda bi, s, pt: (0, 0))]
                 + [page_spec(i) for i in range(g)] * 2,
        out_specs=new,
        scratch_shapes=[pltpu.VMEM((rows, SB_WIDTH), BF16), pltpu.VMEM((rows, page_size), F32),
                        pltpu.VMEM((rows, SB_WIDTH), F32), pltpu.VMEM((rows, page_size), F32)],
    )
    return pl.pallas_call(
        functools.partial(_sb_decode_kernel, n_pages_step=g),
        grid_spec=grid_spec,
        out_shape=jax.ShapeDtypeStruct(q3.shape, BF16),
        compiler_params=_params("parallel", "arbitrary"),
        name="sb_decode",
    )(page_table, bias, q3, kn3, vn3, u, *([ck] * g), *([cv] * g))


def _ret_kernel(qr_ref, kr_ref, vr_ref, g_ref, s0_ref, dm_ref, qd_ref, kd_ref, cd_ref,
                o_ref, s_out_ref, s_scr, *, chunk):
    t = pl.program_id(1)
    tt = qr_ref.shape[0]
    pad = tt < chunk
    n_chunks = 1 if pad else tt // chunk

    @pl.when(t == 0)
    def _():
        s_scr[...] = s0_ref[...]

    def load(ref, rows, cols):
        x = ref[rows, cols]
        if pad:
            x = jnp.concatenate([x.astype(F32), jnp.zeros((chunk - tt, x.shape[1]), F32)], axis=0)
        return x

    for h in range(RET_HEADS):
        cols = slice(h * RET_DK, (h + 1) * RET_DK)
        s = s_scr[h]
        for c in range(n_chunks):
            rows = slice(0, tt) if pad else slice(c * chunk, (c + 1) * chunk)
            qc = load(qr_ref, rows, cols).astype(BF16)
            kc = load(kr_ref, rows, cols)
            vc = load(vr_ref, rows, cols).astype(BF16)
            att = lax.dot_general(qc, kc.astype(BF16), (((1,), (1,)), ((), ())),
                                  preferred_element_type=F32) * dm_ref[h]
            o = jnp.dot(att.astype(BF16), vc, preferred_element_type=F32)
            o = o + jnp.dot(qc, s.astype(BF16), preferred_element_type=F32) * qd_ref[h]
            kdec = (kc.astype(F32) * kd_ref[h]).astype(BF16)
            s = s * cd_ref[h] + lax.dot_general(kdec, vc, (((0,), (0,)), ((), ())),
                                                preferred_element_type=F32)
            mu = jnp.mean(o, axis=-1, keepdims=True)
            d = o - mu
            var = jnp.mean(d * d, axis=-1, keepdims=True)
            on = d * lax.rsqrt(var + LN_EPS)
            gate = g_ref[rows, cols]
            swish = gate * (1.0 / (1.0 + jnp.exp(-gate)))
            o_ref[rows, cols] = ((on[:tt] if pad else on) * swish).astype(BF16)
        s_scr[h] = s

    @pl.when(t == pl.num_programs(1) - 1)
    def _():
        s_out_ref[...] = s_scr[...]


def _retention_tables(chunk, true_chunk):
    lg = jnp.log(1.0 - 2.0 ** (-5.0 - jnp.arange(RET_HEADS, dtype=F32)))
    idx = jnp.arange(chunk, dtype=F32)
    diff = idx[:, None] - idx[None, :]
    dm = jnp.where(diff[None] >= 0, jnp.exp(jnp.maximum(diff, 0.0)[None] * lg[:, None, None]), 0.0)
    qd = jnp.exp((idx[None, :] + 1.0) * lg[:, None])
    kd = jnp.exp((true_chunk - 1.0 - idx)[None, :] * lg[:, None])
    kd = jnp.where(idx[None, :] < true_chunk, kd, 0.0)
    cd = jnp.exp(true_chunk * lg)
    rep = lambda a: jnp.broadcast_to(a[:, :, None], (RET_HEADS, chunk, RET_DK))
    cd = jnp.broadcast_to(cd[:, None, None], (RET_HEADS, SUBLANES, RET_DK))
    return dm, rep(qd), rep(kd), cd[:, :1, :]


def _retention(qr3, kr3, vr3, g3, s0):
    b, t, _ = qr3.shape
    chunk = RET_CHUNK
    true_chunk = chunk if t % chunk == 0 else t
    assert true_chunk <= chunk
    tt = min(ROW_TILE, t)
    dm, qd, kd, cd = _retention_tables(chunk, true_chunk)
    seq = pl.BlockSpec((None, tt, RET_WIDTH), lambda bi, ti: (bi, ti, 0))
    state = pl.BlockSpec((None,) + s0.shape[1:], lambda bi, ti: (bi, 0, 0, 0))
    return pl.pallas_call(
        functools.partial(_ret_kernel, chunk=chunk),
        grid=(b, t // tt),
        in_specs=[seq, seq, seq, seq, state] + [_resident(a.shape) for a in (dm, qd, kd, cd)],
        out_specs=[seq, state],
        out_shape=[jax.ShapeDtypeStruct(qr3.shape, BF16), jax.ShapeDtypeStruct(s0.shape, F32)],
        scratch_shapes=[pltpu.VMEM(s0.shape[1:], F32)],
        compiler_params=_params("parallel", "arbitrary"),
        name="retention",
    )(qr3, kr3, vr3, g3, s0, dm, qd, kd, cd)


def _mix_kernel(osb_ref, oret_ref, x_ref, wo_ref, g_ref, b_ref, h_ref):
    mix = jnp.dot(osb_ref[...], wo_ref[:SB_WIDTH, :], preferred_element_type=F32)
    mix = mix + jnp.dot(oret_ref[...], wo_ref[SB_WIDTH:, :], preferred_element_type=F32)
    h_ref[...] = _layer_norm(ALPHA * x_ref[...] + mix, g_ref[...], b_ref[...])


def _mix(o_sb, o_ret, x2, w_o_bf, ln_g, ln_b):
    n = x2.shape[0]
    tm = min(ROW_TILE, n)
    row = lambda i: (i, 0)
    return pl.pallas_call(
        _mix_kernel,
        grid=(n // tm,),
        in_specs=[pl.BlockSpec((tm, SB_WIDTH), row), pl.BlockSpec((tm, RET_WIDTH), row),
                  pl.BlockSpec((tm, D_MODEL), row), _resident(w_o_bf.shape),
                  _resident(ln_g.shape), _resident(ln_b.shape)],
        out_specs=pl.BlockSpec((tm, D_MODEL), row),
        out_shape=jax.ShapeDtypeStruct(x2.shape, F32),
        compiler_params=_params("parallel"),
        name="mix_ln1",
    )(o_sb, o_ret, x2, w_o_bf, ln_g, ln_b)


def _gelu_tanh(x):
    return x * (0.5 * (1.0 + jnp.tanh(0.7978845608028654 * (x + 0.044715 * (x * x * x)))))


def _ffn_kernel(h_ref, prev_ref, wup_ref, cw_ref, cb_ref, wdn_ref, g_ref, b_ref,
                y_ref, cs_ref, carry_scr, *, n_seq, use_carry):
    rows = h_ref.shape[0]
    tt = rows // n_seq
    if use_carry:
        @pl.when(pl.program_id(1) == 0)
        def _():
            carry_scr[...] = prev_ref[...]
        prev = carry_scr
    else:
        prev = prev_ref

    h = h_ref[...]
    hb = h.astype(BF16)
    cw = D_FF // FFN_CHUNKS
    tpos = lax.broadcasted_iota(jnp.int32, (n_seq, tt, cw), 1)

    def conv_cols(c0):
        u = jnp.dot(hb, wup_ref[:, c0:c0 + cw], preferred_element_type=F32)
        u3 = u.reshape(n_seq, tt, cw)
        p0 = prev[:, 0:1, c0:c0 + cw]
        p1 = prev[:, 1:2, c0:c0 + cw]
        r1 = pltpu.roll(u, 1, axis=0).reshape(n_seq, tt, cw)
        r2 = pltpu.roll(u, 2, axis=0).reshape(n_seq, tt, cw)
        um1 = jnp.where(tpos == 0, p1, r1)
        um2 = jnp.where(tpos == 0, p0, jnp.where(tpos == 1, p1, r2))
        w = cw_ref[:, c0:c0 + cw]
        c = cb_ref[:, c0:c0 + cw] + w[0:1] * um2 + w[1:2] * um1 + w[2:3] * u3
        last = u3[:, tt - 2:tt, :]
        cs_ref[:, :, c0:c0 + cw] = last
        if use_carry:
            carry_scr[:, :, c0:c0 + cw] = last
        return c.reshape(rows, cw)

    f = None
    for j in range(FFN_CHUNKS):
        ca = conv_cols(j * cw)
        cb = conv_cols(D_FF + j * cw)
        gated = (_gelu_tanh(ca) * cb).astype(BF16)
        part = jnp.dot(gated, wdn_ref[j * cw:(j + 1) * cw, :], preferred_element_type=F32)
        f = part if f is None else f + part
    y_ref[...] = _layer_norm(ALPHA * h + f, g_ref[...], b_ref[...])


def _ffn(h3, conv0, w_up_bf, conv_w, conv_b, w_down_bf, ln_g, ln_b):
    b, t, _ = h3.shape
    weights = [_resident(a.shape) for a in (w_up_bf, conv_w, conv_b, w_down_bf, ln_g, ln_b)]
    if t >= FFN_ROW_TILE:
        tm, n_seq = FFN_ROW_TILE, 1
        h_in = h3
        grid = (b, t // tm)
        hspec = pl.BlockSpec((None, tm, D_MODEL), lambda bi, ti: (bi, ti, 0))
        cspec = pl.BlockSpec((1, CONV_W - 1, 2 * D_FF), lambda bi, ti: (bi, 0, 0))
        y_shape = h3.shape
    else:
        n_seq = FFN_ROW_TILE // t
        while b % n_seq:
            n_seq //= 2
        tm = n_seq * t
        h_in = h3.reshape(b * t, D_MODEL)
        grid = (b // n_seq, 1)
        hspec = pl.BlockSpec((tm, D_MODEL), lambda bi, ti: (bi, 0))
        cspec = pl.BlockSpec((n_seq, CONV_W - 1, 2 * D_FF), lambda bi, ti: (bi, 0, 0))
        y_shape = h_in.shape
    y, cs = pl.pallas_call(
        functools.partial(_ffn_kernel, n_seq=n_seq, use_carry=t >= FFN_ROW_TILE),
        grid=grid,
        in_specs=[hspec, cspec] + weights,
        out_specs=[hspec, cspec],
        out_shape=[jax.ShapeDtypeStruct(y_shape, F32), jax.ShapeDtypeStruct(conv0.shape, F32)],
        scratch_shapes=[pltpu.VMEM((n_seq, CONV_W - 1, 2 * D_FF), F32)],
        compiler_params=_params("parallel", "arbitrary"),
        name="conv_ffn",
    )(h_in, conv0, w_up_bf, conv_w, conv_b, w_down_bf, ln_g, ln_b)
    return y.reshape(h3.shape), cs


def _rotation_tables(pos):
    half = RET_DK // 2
    inv = 1.0 / (ROT_BASE ** jnp.linspace(0.0, 1.0, half, dtype=F32))
    ang = pos.astype(F32)[:, None] * inv[None, :]
    cos, sin = jnp.cos(ang), jnp.sin(ang)
    return jnp.concatenate([cos, cos], axis=-1), jnp.concatenate([-sin, sin], axis=-1)


def _hybrid_layer(x, pos0, attend, s_ret0, conv_buf, w):
    b, t, _ = x.shape
    n = b * t
    x2 = x.reshape(n, D_MODEL)
    cos_t, sin_t = _rotation_tables(pos0 + jnp.arange(t))
    tm = min(ROW_TILE, n)
    if t < tm:
        cos_t, sin_t = jnp.tile(cos_t, (tm // t, 1)), jnp.tile(sin_t, (tm // t, 1))
    q, k, v, kb, vb, qr, kr, vr, gate = _proj(x2, w["w_in"], cos_t, sin_t)
    three = lambda a: a.reshape(b, t, a.shape[-1])
    heads = lambda a: a.reshape(b, t, SB_HEADS, SB_HD)
    k, v = heads(k), heads(v)
    o_sb = attend(three(q), three(kb), three(vb), w["sb_bias"])
    o_ret, s_new = _retention(three(qr), three(kr), three(vr), three(gate), s_ret0)
    h = _mix(o_sb.reshape(n, SB_WIDTH), o_ret.reshape(n, RET_WIDTH), x2,
             w["w_o"], w["ln1_g"], w["ln1_b"])
    y, conv_new = _ffn(three(h), conv_buf, w["w_up"], w["conv_w"], w["conv_b"], w["w_down"],
                       w["ln2_g"], w["ln2_b"])
    return y, k, v, s_new, conv_new


def kernel(x_prompt, x_sample, cache_k, cache_v, page_table, state_ret, state_conv, w_in, sb_bias,
           w_o, ln1_g, ln1_b, w_up, conv_w, conv_b, w_down, ln2_g, ln2_b):
    xp, xs = x_prompt, x_sample
    outs = [[] for _ in range(8)]
    for l in range(DEPTH):
        w = dict(w_in=w_in[l].astype(BF16), sb_bias=sb_bias[l], w_o=w_o[l].astype(BF16),
                 ln1_g=ln1_g[l][None], ln1_b=ln1_b[l][None], w_up=w_up[l].astype(BF16),
                 conv_w=conv_w[l], conv_b=conv_b[l][None], w_down=w_down[l].astype(BF16),
                 ln2_g=ln2_g[l][None], ln2_b=ln2_b[l][None])
        bp = xp.shape[0]
        s0 = jnp.zeros((bp, RET_HEADS, RET_DK, RET_DK), F32)
        c0 = jnp.zeros((bp, CONV_W - 1, 2 * D_FF), F32)
        xp, kp, vp, sp, cp = _hybrid_layer(xp, 0, _sb_prompt, s0, c0, w)
        past_len = page_table.shape[1] * cache_k.shape[2]
        attend = functools.partial(_sb_decode, cache_k=cache_k[l], cache_v=cache_v[l],
                                   page_table=page_table)
        xs, kk, vv, ss, cs = _hybrid_layer(xs, past_len, attend, state_ret[l], state_conv[l], w)
        for lst, a in zip(outs, (kp, vp, sp, cp, kk, vv, ss, cs)):
            lst.append(a)
    return (xp, xs) + tuple(jnp.stack(lst) for lst in outs)
```

```python
import functools

import jax
import jax.numpy as jnp
from jax import lax
from jax.experimental import pallas as pl
from jax.experimental.pallas import tpu as pltpu

F32 = jnp.float32
BF16 = jnp.bfloat16

D_MODEL = 1024
DEPTH = 1
SB_HD = 64
SB_WIDTH = D_MODEL // 2
SB_HEADS = SB_WIDTH // SB_HD
SB_SCALE = SB_HD ** -0.5
RET_HEADS = 4
RET_WIDTH = D_MODEL - SB_WIDTH
RET_DK = RET_WIDTH // RET_HEADS
RET_CHUNK = 128
ROT_BASE = 10000.0
PROJ_WIDTH = 3 * SB_WIDTH + 4 * RET_WIDTH
D_FF = ((8 * D_MODEL // 3 + 127) // 128) * 128
CONV_W = 3
ALPHA = (2.0 * DEPTH) ** 0.25
LN_EPS = 1e-5
LOG2E = 1.4426950408889634
MASK_BIAS = -1e30

LANES = 128
SUBLANES = 8
VMEM_LIMIT = 56 * 1024 * 1024

ROW_TILE = 512
FFN_ROW_TILE = 512
FFN_SHORT_ROWS = 256
FFN_CHUNKS = 2
SB_KEY_BLOCK = 256
SB_QUERY_TILE = 512
DEC_PAGES_PER_STEP = 32


def _params(*sem):
    return pltpu.CompilerParams(dimension_semantics=sem, vmem_limit_bytes=VMEM_LIMIT)


def _resident(shape):
    nd = len(shape)
    return pl.BlockSpec(shape, lambda *_: (0,) * nd, pipeline_mode=pl.Buffered(1))


def _layer_norm(x, g, b):
    mu = jnp.mean(x, axis=-1, keepdims=True)
    d = x - mu
    var = jnp.mean(d * d, axis=-1, keepdims=True)
    return d * lax.rsqrt(var + LN_EPS) * g + b


def _proj_kernel(x_ref, w_ref, cos_ref, sin_ref,
                 q_o, k_o, v_o, kb_o, vb_o, qr_o, kr_o, vr_o, g_o):
    xb = x_ref[...].astype(BF16)

    def mm(c0, width):
        return jnp.dot(xb, w_ref[:, c0:c0 + width], preferred_element_type=F32)

    q_o[...] = (mm(0, SB_WIDTH) * (SB_SCALE * LOG2E)).astype(BF16)
    def store_heads(out_ref, a):
        for h in range(SB_HEADS):
            out_ref[pl.ds(h, a.shape[0], stride=SB_HEADS), :] = a[:, h * SB_HD:(h + 1) * SB_HD]

    k = mm(SB_WIDTH, SB_WIDTH)
    store_heads(k_o, k)
    kb_o[...] = k.astype(BF16)
    v = mm(2 * SB_WIDTH, SB_WIDTH)
    store_heads(v_o, v)
    vb_o[...] = v.astype(BF16)

    cos = cos_ref[...]
    sin = sin_ref[...]

    def rotate_into(out_ref, xr, scale):
        for h in range(RET_HEADS):
            xh = xr[:, h * RET_DK:(h + 1) * RET_DK]
            r = xh * cos + pltpu.roll(xh, RET_DK // 2, axis=1) * sin
            if scale is not None:
                r = r * scale
            out_ref[:, h * RET_DK:(h + 1) * RET_DK] = r.astype(BF16)

    base = 3 * SB_WIDTH
    rotate_into(qr_o, mm(base, RET_WIDTH), None)
    rotate_into(kr_o, mm(base + RET_WIDTH, RET_WIDTH), RET_DK ** -0.5)
    vr_o[...] = mm(base + 2 * RET_WIDTH, RET_WIDTH).astype(BF16)
    g_o[...] = mm(base + 3 * RET_WIDTH, RET_WIDTH)


def _proj(x2, w_in_bf, cos_t, sin_t):
    n = x2.shape[0]
    tm = min(ROW_TILE, n)
    n_tab = cos_t.shape[0] // tm
    row = lambda i: (i, 0)
    tab = lambda i: (i % n_tab, 0)
    half = lambda dt: jax.ShapeDtypeStruct((n, SB_WIDTH), dt)
    spec = pl.BlockSpec((tm, SB_WIDTH), row)
    by_head = jax.ShapeDtypeStruct((n * SB_HEADS, SB_HD), F32)
    by_head_spec = pl.BlockSpec((tm * SB_HEADS, SB_HD), row)
    return pl.pallas_call(
        _proj_kernel,
        grid=(n // tm,),
        in_specs=[pl.BlockSpec((tm, D_MODEL), row), _resident(w_in_bf.shape),
                  pl.BlockSpec((tm, RET_DK), tab), pl.BlockSpec((tm, RET_DK), tab)],
        out_specs=[spec, by_head_spec, by_head_spec] + [spec] * 6,
        out_shape=[half(BF16), by_head, by_head, half(BF16), half(BF16),
                   half(BF16), half(BF16), half(BF16), half(F32)],
        compiler_params=_params("parallel"),
        name="proj",
    )(x2, w_in_bf, cos_t, sin_t)


def _neg_abs(x):
    bits = lax.bitcast_convert_type(x, jnp.uint32) | jnp.uint32(0x80000000)
    return lax.bitcast_convert_type(bits, F32)


def _log2_sigmoids(y):
    ls = jnp.minimum(y, 0.0) - jnp.log2(1.0 + jnp.exp2(_neg_abs(y)))
    return ls, ls - y


def _suffix_matrix(n):
    j = lax.broadcasted_iota(jnp.int32, (n, n), 0)
    s = lax.broadcasted_iota(jnp.int32, (n, n), 1)
    return (j > s).astype(BF16)


def _sb_prompt_kernel(bias_ref, q_ref, k_ref, v_ref, u_ref, o_ref, mb_ref, acc_ref, carry_ref,
                      s0, s1, ls0, ls1, hi0, hi1, tot0, tot1, w0, w1, *, tq):
    pair = pl.program_id(1)
    tk = u_ref.shape[0]
    ratio = tq // tk
    nq = q_ref.shape[0] // tq
    n_items = ratio * nq * (nq + 1) // 2
    n_lanes = tk // LANES
    lane = lax.broadcasted_iota(jnp.int32, (tq, LANES), 1)
    low = lane < SB_HD
    s_buf, ls_buf, hi_buf, tot_buf, w_buf = (s0, s1), (ls0, ls1), (hi0, hi1), (tot0, tot1), (w0, w1)

    row = lax.broadcasted_iota(jnp.int32, (tq, tk), 0)
    col = lax.broadcasted_iota(jnp.int32, (tq, tk), 1)
    for h in range(2):
        b2 = bias_ref[2 * pair + h] * LOG2E
        for j in range(ratio):
            causal = col + (ratio - 1 - j) * tk < row
            mb_ref[h, j] = jnp.where(causal, 0.0, MASK_BIAS) + b2
        mb_ref[h, ratio] = jnp.zeros((tq, tk), F32) + b2
    for ref in (acc_ref, carry_ref) + s_buf + ls_buf + hi_buf + tot_buf + w_buf:
        ref[...] = jnp.zeros_like(ref)

    def key_block(qi, j):
        return pl.ds(pl.multiple_of(((qi + 1) * ratio - 1 - j) * tk, tk), tk)

    def step(t, items, p):
        (qi0, j0), (_, j1), (_, j2), (qi3, j3) = items
        wr, rd = p, 1 - p

        def scores():
            q = q_ref[pl.ds(pl.multiple_of(qi0 * tq, tq), tq), :]
            k = k_ref[key_block(qi0, j0), :]
            zero = jnp.zeros_like(q)
            for h, qh in enumerate((jnp.where(low, q, zero), jnp.where(low, zero, q))):
                s_buf[wr][h] = lax.dot_general(qh, k, (((1,), (1,)), ((), ())),
                                               preferred_element_type=F32)

        def logits():
            variant = jnp.minimum(j1, ratio)
            for h in range(2):
                ls, lstay = _log2_sigmoids(s_buf[rd][h] + mb_ref[h, variant])
                ls_buf[wr][h] = ls
                hi_buf[wr][h] = lstay.astype(BF16)
                tot_buf[wr][h] = jnp.broadcast_to(jnp.sum(lstay, axis=1, keepdims=True),
                                                  (tq, LANES))

        def weights():
            for h in range(2):
                suf = jnp.dot(hi_buf[rd][h], u_ref[...], preferred_element_type=F32)
                c = carry_ref[h]
                cc = jnp.concatenate([c] * n_lanes, axis=1)
                w_buf[wr][h] = jnp.exp2(ls_buf[rd][h] + suf + cc).astype(BF16)
                carry_ref[h] = c + tot_buf[rd][h]

        def outputs():
            v = v_ref[key_block(qi3, j3), :]
            for h in range(2):
                acc_ref[h] += jnp.dot(w_buf[rd][h], v, preferred_element_type=F32)

        outputs()
        scores()
        logits()
        weights()

        @pl.when((t >= 3) & (t - 3 < n_items) & (j3 == (qi3 + 1) * ratio - 1))
        def _():
            o_ref[pl.ds(pl.multiple_of(qi3 * tq, tq), tq), :] = (
                jnp.where(low, acc_ref[0], acc_ref[1]).astype(BF16))

        @pl.when(j2 == 0)
        def _():
            acc_ref[...] = jnp.zeros_like(acc_ref)

        @pl.when(j1 == 0)
        def _():
            carry_ref[...] = jnp.zeros_like(carry_ref)

        wrap = j0 == (qi0 + 1) * ratio - 1
        done = wrap & (qi0 == nq - 1)
        nxt = (jnp.where(wrap & ~done, qi0 + 1, qi0), jnp.where(done, j0, jnp.where(wrap, 0, j0 + 1)))
        return (nxt,) + items[:3]

    def body(i, items):
        return step(2 * i + 1, step(2 * i, items, 0), 1)

    zero_item = (jnp.int32(0), jnp.int32(0))
    lax.fori_loop(0, (n_items + 3 + 1) // 2, body, (zero_item,) * 4)


def _sb_prompt(q3, k3, v3, bias):
    b, t, _ = q3.shape
    tk = SB_KEY_BLOCK
    tq = SB_QUERY_TILE if t % SB_QUERY_TILE == 0 else tk
    n_pairs = SB_WIDTH // LANES
    u = _suffix_matrix(tk)
    spec = pl.BlockSpec((None, t, LANES), lambda bi, p: (bi, 0, p))
    f32 = lambda n: pltpu.VMEM((2, tq, n), F32)
    bf16 = lambda n: pltpu.VMEM((2, tq, n), BF16)
    return pl.pallas_call(
        functools.partial(_sb_prompt_kernel, tq=tq),
        grid=(b, n_pairs),
        in_specs=[pl.BlockSpec(memory_space=pltpu.SMEM), spec, spec, spec, _resident(u.shape)],
        out_specs=spec,
        out_shape=jax.ShapeDtypeStruct(q3.shape, BF16),
        scratch_shapes=[pltpu.VMEM((2, tq // tk + 1, tq, tk), F32), f32(LANES), f32(LANES),
                        f32(tk), f32(tk), f32(tk), f32(tk), bf16(tk), bf16(tk),
                        f32(LANES), f32(LANES), bf16(tk), bf16(tk)],
        compiler_params=_params("parallel", "parallel"),
        name="sb_prompt",
    )(bias, q3, k3, v3, u)


def _sb_decode_kernel(pt_ref, bias_ref, q_ref, kn_ref, vn_ref, u_ref, *rest, n_pages_step):
    del pt_ref
    g = n_pages_step
    k_refs, v_refs = rest[:g], rest[g:2 * g]
    o_ref, qbd_ref, biasv_ref, acc_ref, carry_ref = rest[2 * g:]
    step = pl.program_id(1)
    t_new = q_ref.shape[0]
    rows, page = biasv_ref.shape
    contract_last = (((1,), (1,)), ((), ()))

    def weights(y, mask):
        n_blk = y.shape[1] // page
        logs = []
        for i in range(n_blk):
            yi = y[:, i * page:(i + 1) * page] + biasv_ref[...]
            if mask is not None:
                yi = jnp.where(mask, yi, MASK_BIAS)
            logs.append(_log2_sigmoids(yi))
        stay = jnp.concatenate([lstay.astype(BF16) for _, lstay in logs], axis=0)
        suf = jnp.dot(stay, u_ref[...], preferred_element_type=F32)
        carry = carry_ref[...]
        ws = [None] * n_blk
        for i in reversed(range(n_blk)):
            ls, lstay = logs[i]
            ws[i] = jnp.exp2(ls + suf[i * rows:(i + 1) * rows] + carry).astype(BF16)
            carry = carry + jnp.sum(lstay, axis=1, keepdims=True)
        carry_ref[...] = carry
        return jnp.concatenate(ws, axis=1)

    @pl.when(step == 0)
    def _init():
        q = q_ref[...].astype(F32)
        q_rep = jnp.concatenate([q] * SB_HEADS, axis=0)
        r = lax.broadcasted_iota(jnp.int32, (rows, SB_WIDTH), 0) // t_new
        c = lax.broadcasted_iota(jnp.int32, (rows, SB_WIDTH), 1) // SB_HD
        qbd_ref[...] = jnp.where(r == c, q_rep, 0.0).astype(BF16)
        rh = lax.broadcasted_iota(jnp.int32, (rows, page), 0) // t_new
        bv = jnp.zeros((rows, page), F32)
        for h in range(SB_HEADS):
            bv = jnp.where(rh == h, bias_ref[h] * LOG2E, bv)
        biasv_ref[...] = bv
        acc_ref[...] = jnp.zeros_like(acc_ref)
        carry_ref[...] = jnp.zeros_like(carry_ref)
        pad = jnp.zeros((page - t_new, SB_WIDTH), F32)
        kn = jnp.concatenate([kn_ref[...].astype(F32), pad], axis=0).astype(BF16)
        vn = jnp.concatenate([vn_ref[...].astype(F32), pad], axis=0).astype(BF16)
        i = lax.broadcasted_iota(jnp.int32, (rows, page), 0) % t_new
        j = lax.broadcasted_iota(jnp.int32, (rows, page), 1)
        y = lax.dot_general(qbd_ref[...], kn, contract_last, preferred_element_type=F32)
        acc_ref[...] += jnp.dot(weights(y, j < i), vn, preferred_element_type=F32)

    k_t = jnp.concatenate([r[...] for r in k_refs], axis=1).astype(BF16)
    y = jnp.dot(qbd_ref[...], k_t, preferred_element_type=F32)
    w = weights(y, None)
    v_t = jnp.concatenate([r[...] for r in v_refs], axis=1).astype(BF16)
    acc_ref[...] += lax.dot_general(w, v_t, contract_last, preferred_element_type=F32)

    @pl.when(step == pl.num_programs(1) - 1)
    def _finish():
        lane = lax.broadcasted_iota(jnp.int32, (t_new, LANES), 1)
        for p in range(SB_WIDTH // LANES):
            cols = slice(p * LANES, (p + 1) * LANES)
            even = acc_ref[2 * p * t_new:(2 * p + 1) * t_new, cols]
            odd = acc_ref[(2 * p + 1) * t_new:(2 * p + 2) * t_new, cols]
            o_ref[:, cols] = jnp.where(lane < SB_HD, even, odd).astype(BF16)


def _sb_decode(q3, kn3, vn3, bias, cache_k, cache_v, page_table):
    b, t_new, _ = q3.shape
    n_pages = page_table.shape[1]
    n_phys, page_size = cache_k.shape[:2]
    assert page_size == LANES and t_new == SUBLANES
    g = DEC_PAGES_PER_STEP
    while n_pages % g:
        g //= 2
    n_steps = n_pages // g
    ck = jnp.transpose(cache_k, (0, 2, 3, 1)).reshape(n_phys, SB_WIDTH, page_size)
    cv = jnp.transpose(cache_v, (0, 2, 3, 1)).reshape(n_phys, SB_WIDTH, page_size)
    u = _suffix_matrix(page_size)
    rows = SB_HEADS * t_new
    new = pl.BlockSpec((None, t_new, SB_WIDTH), lambda bi, s, pt: (bi, 0, 0))

    def page_spec(i):
        return pl.BlockSpec((None, SB_WIDTH, page_size),
                            lambda bi, s, pt: (pt[bi, (n_steps - 1 - s) * g + i], 0, 0))

    grid_spec = pltpu.PrefetchScalarGridSpec(
        num_scalar_prefetch=1,
        grid=(b, n_steps),
        in_specs=[pl.BlockSpec(memory_space=pltpu.SMEM), new, new, new,
                  pl.BlockSpec(u.shape, lambda bi, s, pt: (0, 0))]
                 + [page_spec(i) for i in range(g)] * 2,
        out_specs=new,
        scratch_shapes=[pltpu.VMEM((rows, SB_WIDTH), BF16), pltpu.VMEM((rows, page_size), F32),
                        pltpu.VMEM((rows, SB_WIDTH), F32), pltpu.VMEM((rows, page_size), F32)],
    )
    return pl.pallas_call(
        functools.partial(_sb_decode_kernel, n_pages_step=g),
        grid_spec=grid_spec,
        out_shape=jax.ShapeDtypeStruct(q3.shape, BF16),
        compiler_params=_params("parallel", "arbitrary"),
        name="sb_decode",
    )(page_table, bias, q3, kn3, vn3, u, *([ck] * g), *([cv] * g))


def _ret_kernel(qr_ref, kr_ref, vr_ref, g_ref, s0_ref, dm_ref, qd_ref, kd_ref, cd_ref,
                o_ref, s_out_ref, s_scr, *, chunk):
    t = pl.program_id(1)
    tt = qr_ref.shape[0]
    pad = tt < chunk
    n_chunks = 1 if pad else tt // chunk

    @pl.when(t == 0)
    def _():
        s_scr[...] = s0_ref[...]

    def load(ref, rows, cols):
        x = ref[rows, cols]
        if pad:
            x = jnp.concatenate([x.astype(F32), jnp.zeros((chunk - tt, x.shape[1]), F32)], axis=0)
        return x

    pairs = [(h, c) for h in range(RET_HEADS) for c in range(n_chunks)]
    cols = lambda h: slice(h * RET_DK, (h + 1) * RET_DK)
    rows = lambda c: slice(0, tt) if pad else slice(c * chunk, (c + 1) * chunk)
    q = {hc: load(qr_ref, rows(hc[1]), cols(hc[0])).astype(BF16) for hc in pairs}
    k = {hc: load(kr_ref, rows(hc[1]), cols(hc[0])) for hc in pairs}
    v = {hc: load(vr_ref, rows(hc[1]), cols(hc[0])).astype(BF16) for hc in pairs}
    att = {hc: lax.dot_general(q[hc], k[hc].astype(BF16), (((1,), (1,)), ((), ())),
                               preferred_element_type=F32) * dm_ref[hc[0]] for hc in pairs}
    kv = {hc: lax.dot_general((k[hc].astype(F32) * kd_ref[hc[0]]).astype(BF16), v[hc],
                              (((0,), (0,)), ((), ())), preferred_element_type=F32)
          for hc in pairs}
    state = {}
    for h in range(RET_HEADS):
        s = s_scr[h]
        for c in range(n_chunks):
            state[h, c] = s
            s = s * cd_ref[h] + kv[h, c]
        s_scr[h] = s
    for h, c in pairs:
        o = jnp.dot(att[h, c].astype(BF16), v[h, c], preferred_element_type=F32)
        o = o + jnp.dot(q[h, c], state[h, c].astype(BF16), preferred_element_type=F32) * qd_ref[h]
        mu = jnp.mean(o, axis=-1, keepdims=True)
        d = o - mu
        var = jnp.mean(d * d, axis=-1, keepdims=True)
        on = d * lax.rsqrt(var + LN_EPS)
        gate = g_ref[rows(c), cols(h)]
        swish = gate * (1.0 / (1.0 + jnp.exp(-gate)))
        o_ref[rows(c), cols(h)] = ((on[:tt] if pad else on) * swish).astype(BF16)

    @pl.when(t == pl.num_programs(1) - 1)
    def _():
        s_out_ref[...] = s_scr[...]


def _retention_tables(chunk, true_chunk):
    lg = jnp.log(1.0 - 2.0 ** (-5.0 - jnp.arange(RET_HEADS, dtype=F32)))
    idx = jnp.arange(chunk, dtype=F32)
    diff = idx[:, None] - idx[None, :]
    dm = jnp.where(diff[None] >= 0, jnp.exp(jnp.maximum(diff, 0.0)[None] * lg[:, None, None]), 0.0)
    qd = jnp.exp((idx[None, :] + 1.0) * lg[:, None])
    kd = jnp.exp((true_chunk - 1.0 - idx)[None, :] * lg[:, None])
    kd = jnp.where(idx[None, :] < true_chunk, kd, 0.0)
    cd = jnp.exp(true_chunk * lg)
    rep = lambda a: jnp.broadcast_to(a[:, :, None], (RET_HEADS, chunk, RET_DK))
    cd = jnp.broadcast_to(cd[:, None, None], (RET_HEADS, SUBLANES, RET_DK))
    return dm, rep(qd), rep(kd), cd[:, :1, :]


def _retention(qr3, kr3, vr3, g3, s0):
    b, t, _ = qr3.shape
    chunk = RET_CHUNK
    true_chunk = chunk if t % chunk == 0 else t
    assert true_chunk <= chunk
    tt = min(ROW_TILE, t)
    dm, qd, kd, cd = _retention_tables(chunk, true_chunk)
    seq = pl.BlockSpec((None, tt, RET_WIDTH), lambda bi, ti: (bi, ti, 0))
    state = pl.BlockSpec((None,) + s0.shape[1:], lambda bi, ti: (bi, 0, 0, 0))
    return pl.pallas_call(
        functools.partial(_ret_kernel, chunk=chunk),
        grid=(b, t // tt),
        in_specs=[seq, seq, seq, seq, state] + [_resident(a.shape) for a in (dm, qd, kd, cd)],
        out_specs=[seq, state],
        out_shape=[jax.ShapeDtypeStruct(qr3.shape, BF16), jax.ShapeDtypeStruct(s0.shape, F32)],
        scratch_shapes=[pltpu.VMEM(s0.shape[1:], F32)],
        compiler_params=_params("parallel", "arbitrary"),
        name="retention",
    )(qr3, kr3, vr3, g3, s0, dm, qd, kd, cd)


def _gelu_tanh(x):
    return x * (0.5 * (1.0 + jnp.tanh(0.7978845608028654 * (x + 0.044715 * (x * x * x)))))


def _post_kernel(osb_ref, oret_ref, x_ref, prev_ref, wo_ref, g1_ref, b1_ref, wup_ref, cw_ref,
                 cb_ref, wdn_ref, g_ref, b_ref, y_ref, cs_ref, carry_scr, *, n_seq, use_carry):
    rows = x_ref.shape[0]
    tt = rows // n_seq
    if use_carry:
        @pl.when(pl.program_id(1) == 0)
        def _():
            carry_scr[...] = prev_ref[...]
        prev = carry_scr
    else:
        prev = prev_ref

    mix = jnp.dot(osb_ref[...], wo_ref[:SB_WIDTH, :], preferred_element_type=F32)
    mix = mix + jnp.dot(oret_ref[...], wo_ref[SB_WIDTH:, :], preferred_element_type=F32)
    h = _layer_norm(ALPHA * x_ref[...] + mix, g1_ref[...], b1_ref[...])
    hb = h.astype(BF16)
    cw = D_FF // FFN_CHUNKS
    tpos = lax.broadcasted_iota(jnp.int32, (n_seq, tt, cw), 1)

    def conv_cols(c0):
        u = jnp.dot(hb, wup_ref[:, c0:c0 + cw], preferred_element_type=F32)
        u3 = u.reshape(n_seq, tt, cw)
        p0 = prev[:, 0:1, c0:c0 + cw]
        p1 = prev[:, 1:2, c0:c0 + cw]
        r1 = pltpu.roll(u, 1, axis=0).reshape(n_seq, tt, cw)
        r2 = pltpu.roll(u, 2, axis=0).reshape(n_seq, tt, cw)
        um1 = jnp.where(tpos == 0, p1, r1)
        um2 = jnp.where(tpos == 0, p0, jnp.where(tpos == 1, p1, r2))
        w = cw_ref[:, c0:c0 + cw]
        c = cb_ref[:, c0:c0 + cw] + w[0:1] * um2 + w[1:2] * um1 + w[2:3] * u3
        last = u3[:, tt - 2:tt, :]
        cs_ref[:, :, c0:c0 + cw] = last
        if use_carry:
            carry_scr[:, :, c0:c0 + cw] = last
        return c.reshape(rows, cw)

    f = None
    for j in range(FFN_CHUNKS):
        ca = conv_cols(j * cw)
        cb = conv_cols(D_FF + j * cw)
        gated = (_gelu_tanh(ca) * cb).astype(BF16)
        part = jnp.dot(gated, wdn_ref[j * cw:(j + 1) * cw, :], preferred_element_type=F32)
        f = part if f is None else f + part
    y_ref[...] = _layer_norm(ALPHA * h + f, g_ref[...], b_ref[...])


def _post(o_sb3, o_ret3, x3, conv0, w):
    b, t, _ = x3.shape
    weights = [w[name] for name in ("w_o", "ln1_g", "ln1_b", "w_up", "conv_w", "conv_b", "w_down",
                                    "ln2_g", "ln2_b")]
    if t >= FFN_ROW_TILE:
        tm, n_seq = FFN_ROW_TILE, 1
        acts = (o_sb3, o_ret3, x3)
        grid = (b, t // tm)
        row_spec = lambda width: pl.BlockSpec((None, tm, width), lambda bi, ti: (bi, ti, 0))
        cspec = pl.BlockSpec((1, CONV_W - 1, 2 * D_FF), lambda bi, ti: (bi, 0, 0))
    else:
        n_seq = FFN_SHORT_ROWS // t
        while b % n_seq:
            n_seq //= 2
        tm = n_seq * t
        acts = tuple(a.reshape(b * t, a.shape[-1]) for a in (o_sb3, o_ret3, x3))
        grid = (b // n_seq, 1)
        row_spec = lambda width: pl.BlockSpec((tm, width), lambda bi, ti: (bi, 0))
        cspec = pl.BlockSpec((n_seq, CONV_W - 1, 2 * D_FF), lambda bi, ti: (bi, 0, 0))
    y, cs = pl.pallas_call(
        functools.partial(_post_kernel, n_seq=n_seq, use_carry=t >= FFN_ROW_TILE),
        grid=grid,
        in_specs=[row_spec(SB_WIDTH), row_spec(RET_WIDTH), row_spec(D_MODEL), cspec]
                 + [_resident(a.shape) for a in weights],
        out_specs=[row_spec(D_MODEL), cspec],
        out_shape=[jax.ShapeDtypeStruct(acts[2].shape, F32), jax.ShapeDtypeStruct(conv0.shape, F32)],
        scratch_shapes=[pltpu.VMEM((n_seq, CONV_W - 1, 2 * D_FF), F32)],
        compiler_params=_params("parallel", "arbitrary"),
        name="mix_conv_ffn",
    )(*acts, conv0, *weights)
    return y.reshape(x3.shape), cs


def _rotation_tables(pos):
    half = RET_DK // 2
    inv = 1.0 / (ROT_BASE ** jnp.linspace(0.0, 1.0, half, dtype=F32))
    ang = pos.astype(F32)[:, None] * inv[None, :]
    cos, sin = jnp.cos(ang), jnp.sin(ang)
    return jnp.concatenate([cos, cos], axis=-1), jnp.concatenate([-sin, sin], axis=-1)


def _hybrid_layer(x, pos0, attend, s_ret0, conv_buf, w):
    b, t, _ = x.shape
    n = b * t
    x2 = x.reshape(n, D_MODEL)
    cos_t, sin_t = _rotation_tables(pos0 + jnp.arange(t))
    tm = min(ROW_TILE, n)
    if t < tm:
        cos_t, sin_t = jnp.tile(cos_t, (tm // t, 1)), jnp.tile(sin_t, (tm // t, 1))
    q, k, v, kb, vb, qr, kr, vr, gate = _proj(x2, w["w_in"], cos_t, sin_t)
    three = lambda a: a.reshape(b, t, a.shape[-1])
    heads = lambda a: a.reshape(b, t, SB_HEADS, SB_HD)
    k, v = heads(k), heads(v)
    o_sb = attend(three(q), three(kb), three(vb), w["sb_bias"])
    o_ret, s_new = _retention(three(qr), three(kr), three(vr), three(gate), s_ret0)
    y, conv_new = _post(o_sb, o_ret, x, conv_buf, w)
    return y, k, v, s_new, conv_new


def kernel(x_prompt, x_sample, cache_k, cache_v, page_table, state_ret, state_conv, w_in, sb_bias,
           w_o, ln1_g, ln1_b, w_up, conv_w, conv_b, w_down, ln2_g, ln2_b):
    xp, xs = x_prompt, x_sample
    outs = [[] for _ in range(8)]
    for l in range(DEPTH):
        w = dict(w_in=w_in[l].astype(BF16), sb_bias=sb_bias[l], w_o=w_o[l].astype(BF16),
                 ln1_g=ln1_g[l][None], ln1_b=ln1_b[l][None], w_up=w_up[l].astype(BF16),
                 conv_w=conv_w[l], conv_b=conv_b[l][None], w_down=w_down[l].astype(BF16),
                 ln2_g=ln2_g[l][None], ln2_b=ln2_b[l][None])
        bp = xp.shape[0]
        s0 = jnp.zeros((bp, RET_HEADS, RET_DK, RET_DK), F32)
        c0 = jnp.zeros((bp, CONV_W - 1, 2 * D_FF), F32)
        xp, kp, vp, sp, cp = _hybrid_layer(xp, 0, _sb_prompt, s0, c0, w)
        past_len = page_table.shape[1] * cache_k.shape[2]
        attend = functools.partial(_sb_decode, cache_k=cache_k[l], cache_v=cache_v[l],
                                   page_table=page_table)
        xs, kk, vv, ss, cs = _hybrid_layer(xs, past_len, attend, state_ret[l], state_conv[l], w)
        for lst, a in zip(outs, (kp, vp, sp, cp, kk, vv, ss, cs)):
            lst.append(a)
    return (xp, xs) + tuple(jnp.stack(lst) for lst in outs)
```

```python
import functools

import jax
import jax.numpy as jnp
from jax import lax
from jax.experimental import pallas as pl
from jax.experimental.pallas import tpu as pltpu

F32 = jnp.float32
BF16 = jnp.bfloat16

D_MODEL = 1024
DEPTH = 1
SB_HD = 64
SB_WIDTH = D_MODEL // 2
SB_HEADS = SB_WIDTH // SB_HD
SB_SCALE = SB_HD ** -0.5
RET_HEADS = 4
RET_WIDTH = D_MODEL - SB_WIDTH
RET_DK = RET_WIDTH // RET_HEADS
RET_CHUNK = 128
ROT_BASE = 10000.0
PROJ_WIDTH = 3 * SB_WIDTH + 4 * RET_WIDTH
D_FF = ((8 * D_MODEL // 3 + 127) // 128) * 128
CONV_W = 3
ALPHA = (2.0 * DEPTH) ** 0.25
LN_EPS = 1e-5
LOG2E = 1.4426950408889634
MASK_BIAS = -1e30

LANES = 128
SUBLANES = 8
VMEM_LIMIT = 56 * 1024 * 1024

ROW_TILE = 512
FFN_ROW_TILE = 512
FFN_SHORT_ROWS = 256
FFN_CHUNKS = 2
SB_KEY_BLOCK = 256
SB_QUERY_TILE = 512
DEC_PAGES_PER_STEP = 32


def _params(*sem):
    return pltpu.CompilerParams(dimension_semantics=sem, vmem_limit_bytes=VMEM_LIMIT)


def _resident(shape):
    nd = len(shape)
    return pl.BlockSpec(shape, lambda *_: (0,) * nd, pipeline_mode=pl.Buffered(1))


def _layer_norm(x, g, b):
    mu = jnp.mean(x, axis=-1, keepdims=True)
    d = x - mu
    var = jnp.mean(d * d, axis=-1, keepdims=True)
    return d * lax.rsqrt(var + LN_EPS) * g + b


def _proj_kernel(x_ref, w_ref, cos_ref, sin_ref,
                 q_o, k_o, v_o, kb_o, vb_o, qr_o, kr_o, vr_o, g_o):
    xb = x_ref[...].astype(BF16)

    def mm(c0, width):
        return jnp.dot(xb, w_ref[:, c0:c0 + width], preferred_element_type=F32)

    q_o[...] = (mm(0, SB_WIDTH) * (SB_SCALE * LOG2E)).astype(BF16)
    def store_heads(out_ref, a):
        for h in range(SB_HEADS):
            out_ref[pl.ds(h, a.shape[0], stride=SB_HEADS), :] = a[:, h * SB_HD:(h + 1) * SB_HD]

    k = mm(SB_WIDTH, SB_WIDTH)
    store_heads(k_o, k)
    kb_o[...] = k.astype(BF16)
    v = mm(2 * SB_WIDTH, SB_WIDTH)
    store_heads(v_o, v)
    vb_o[...] = v.astype(BF16)

    cos = cos_ref[...]
    sin = sin_ref[...]

    def rotate_into(out_ref, xr, scale):
        for h in range(RET_HEADS):
            xh = xr[:, h * RET_DK:(h + 1) * RET_DK]
            r = xh * cos + pltpu.roll(xh, RET_DK // 2, axis=1) * sin
            if scale is not None:
                r = r * scale
            out_ref[:, h * RET_DK:(h + 1) * RET_DK] = r.astype(BF16)

    base = 3 * SB_WIDTH
    rotate_into(qr_o, mm(base, RET_WIDTH), None)
    rotate_into(kr_o, mm(base + RET_WIDTH, RET_WIDTH), RET_DK ** -0.5)
    vr_o[...] = mm(base + 2 * RET_WIDTH, RET_WIDTH).astype(BF16)
    g_o[...] = mm(base + 3 * RET_WIDTH, RET_WIDTH)


def _proj(x2, w_in_bf, cos_t, sin_t):
    n = x2.shape[0]
    tm = min(ROW_TILE, n)
    n_tab = cos_t.shape[0] // tm
    row = lambda i: (i, 0)
    tab = lambda i: (i % n_tab, 0)
    half = lambda dt: jax.ShapeDtypeStruct((n, SB_WIDTH), dt)
    spec = pl.BlockSpec((tm, SB_WIDTH), row)
    by_head = jax.ShapeDtypeStruct((n * SB_HEADS, SB_HD), F32)
    by_head_spec = pl.BlockSpec((tm * SB_HEADS, SB_HD), row)
    return pl.pallas_call(
        _proj_kernel,
        grid=(n // tm,),
        in_specs=[pl.BlockSpec((tm, D_MODEL), row), _resident(w_in_bf.shape),
                  pl.BlockSpec((tm, RET_DK), tab), pl.BlockSpec((tm, RET_DK), tab)],
        out_specs=[spec, by_head_spec, by_head_spec] + [spec] * 6,
        out_shape=[half(BF16), by_head, by_head, half(BF16), half(BF16),
                   half(BF16), half(BF16), half(BF16), half(F32)],
        compiler_params=_params("parallel"),
        name="proj",
    )(x2, w_in_bf, cos_t, sin_t)


def _neg_abs(x):
    bits = lax.bitcast_convert_type(x, jnp.uint32) | jnp.uint32(0x80000000)
    return lax.bitcast_convert_type(bits, F32)


def _log2_sigmoids(y):
    ls = jnp.minimum(y, 0.0) - jnp.log2(1.0 + jnp.exp2(_neg_abs(y)))
    return ls, ls - y


def _suffix_matrix(n):
    j = lax.broadcasted_iota(jnp.int32, (n, n), 0)
    s = lax.broadcasted_iota(jnp.int32, (n, n), 1)
    return (j > s).astype(BF16)


def _sb_prompt_kernel(bias_ref, q_ref, k_ref, v_ref, u_ref, o_ref, mb_ref, acc_ref, carry_ref,
                      s0, s1, ls0, ls1, hi0, hi1, tot0, tot1, w0, w1, *, tq):
    pair = pl.program_id(1)
    tk = u_ref.shape[0]
    ratio = tq // tk
    nq = q_ref.shape[0] // tq
    n_items = ratio * nq * (nq + 1) // 2
    n_lanes = tk // LANES
    lane = lax.broadcasted_iota(jnp.int32, (tq, LANES), 1)
    low = lane < SB_HD
    s_buf, ls_buf, hi_buf, tot_buf, w_buf = (s0, s1), (ls0, ls1), (hi0, hi1), (tot0, tot1), (w0, w1)

    row = lax.broadcasted_iota(jnp.int32, (tq, tk), 0)
    col = lax.broadcasted_iota(jnp.int32, (tq, tk), 1)
    for h in range(2):
        b2 = bias_ref[2 * pair + h] * LOG2E
        for j in range(ratio):
            causal = col + (ratio - 1 - j) * tk < row
            mb_ref[h, j] = jnp.where(causal, 0.0, MASK_BIAS) + b2
        mb_ref[h, ratio] = jnp.zeros((tq, tk), F32) + b2
    for ref in (acc_ref, carry_ref) + s_buf + ls_buf + hi_buf + tot_buf + w_buf:
        ref[...] = jnp.zeros_like(ref)

    def key_block(qi, j):
        return pl.ds(pl.multiple_of(((qi + 1) * ratio - 1 - j) * tk, tk), tk)

    def step(t, items, p):
        (qi0, j0), (_, j1), (_, j2), (qi3, j3) = items
        wr, rd = p, 1 - p

        def scores():
            q = q_ref[pl.ds(pl.multiple_of(qi0 * tq, tq), tq), :]
            k = k_ref[key_block(qi0, j0), :]
            zero = jnp.zeros_like(q)
            for h, qh in enumerate((jnp.where(low, q, zero), jnp.where(low, zero, q))):
                s_buf[wr][h] = lax.dot_general(qh, k, (((1,), (1,)), ((), ())),
                                               preferred_element_type=F32)

        def logits():
            variant = jnp.minimum(j1, ratio)
            for h in range(2):
                ls, lstay = _log2_sigmoids(s_buf[rd][h] + mb_ref[h, variant])
                ls_buf[wr][h] = ls
                hi_buf[wr][h] = lstay.astype(BF16)
                tot_buf[wr][h] = jnp.broadcast_to(jnp.sum(lstay, axis=1, keepdims=True),
                                                  (tq, LANES))

        def weights():
            for h in range(2):
                suf = jnp.dot(hi_buf[rd][h], u_ref[...], preferred_element_type=F32)
                c = carry_ref[h]
                cc = jnp.concatenate([c] * n_lanes, axis=1)
                w_buf[wr][h] = jnp.exp2(ls_buf[rd][h] + suf + cc).astype(BF16)
                carry_ref[h] = c + tot_buf[rd][h]

        def outputs():
            v = v_ref[key_block(qi3, j3), :]
            zero = jnp.zeros_like(v)
            low_k = lax.broadcasted_iota(jnp.int32, v.shape, 1) < SB_HD
            v2 = jnp.concatenate([jnp.where(low_k, v, zero), jnp.where(low_k, zero, v)], axis=0)
            w2 = jnp.concatenate([w_buf[rd][0], w_buf[rd][1]], axis=1)
            acc_ref[...] += jnp.dot(w2, v2, preferred_element_type=F32)

        outputs()
        scores()
        logits()
        weights()

        @pl.when((t >= 3) & (t - 3 < n_items) & (j3 == (qi3 + 1) * ratio - 1))
        def _():
            o_ref[pl.ds(pl.multiple_of(qi3 * tq, tq), tq), :] = (
                acc_ref[...].astype(BF16))

        @pl.when(j2 == 0)
        def _():
            acc_ref[...] = jnp.zeros_like(acc_ref)

        @pl.when(j1 == 0)
        def _():
            carry_ref[...] = jnp.zeros_like(carry_ref)

        wrap = j0 == (qi0 + 1) * ratio - 1
        done = wrap & (qi0 == nq - 1)
        nxt = (jnp.where(wrap & ~done, qi0 + 1, qi0), jnp.where(done, j0, jnp.where(wrap, 0, j0 + 1)))
        return (nxt,) + items[:3]

    def body(i, items):
        return step(2 * i + 1, step(2 * i, items, 0), 1)

    zero_item = (jnp.int32(0), jnp.int32(0))
    lax.fori_loop(0, (n_items + 3 + 1) // 2, body, (zero_item,) * 4)


def _sb_prompt(q3, k3, v3, bias):
    b, t, _ = q3.shape
    tk = SB_KEY_BLOCK
    tq = SB_QUERY_TILE if t % SB_QUERY_TILE == 0 else tk
    n_pairs = SB_WIDTH // LANES
    u = _suffix_matrix(tk)
    spec = pl.BlockSpec((None, t, LANES), lambda bi, p: (bi, 0, p))
    f32 = lambda n: pltpu.VMEM((2, tq, n), F32)
    bf16 = lambda n: pltpu.VMEM((2, tq, n), BF16)
    return pl.pallas_call(
        functools.partial(_sb_prompt_kernel, tq=tq),
        grid=(b, n_pairs),
        in_specs=[pl.BlockSpec(memory_space=pltpu.SMEM), spec, spec, spec, _resident(u.shape)],
        out_specs=spec,
        out_shape=jax.ShapeDtypeStruct(q3.shape, BF16),
        scratch_shapes=[pltpu.VMEM((2, tq // tk + 1, tq, tk), F32), pltpu.VMEM((tq, LANES), F32),
                        f32(LANES),
                        f32(tk), f32(tk), f32(tk), f32(tk), bf16(tk), bf16(tk),
                        f32(LANES), f32(LANES), bf16(tk), bf16(tk)],
        compiler_params=_params("parallel", "parallel"),
        name="sb_prompt",
    )(bias, q3, k3, v3, u)


def _sb_decode_kernel(pt_ref, bias_ref, q_ref, kn_ref, vn_ref, u_ref, *rest, n_pages_step):
    del pt_ref
    g = n_pages_step
    k_refs, v_refs = rest[:g], rest[g:2 * g]
    o_ref, qbd_ref, biasv_ref, acc_ref, carry_ref = rest[2 * g:]
    step = pl.program_id(1)
    t_new = q_ref.shape[0]
    rows, page = biasv_ref.shape
    contract_last = (((1,), (1,)), ((), ()))

    def weights(y, mask):
        n_blk = y.shape[1] // page
        logs = []
        for i in range(n_blk):
            yi = y[:, i * page:(i + 1) * page] + biasv_ref[...]
            if mask is not None:
                yi = jnp.where(mask, yi, MASK_BIAS)
            logs.append(_log2_sigmoids(yi))
        stay = jnp.concatenate([lstay.astype(BF16) for _, lstay in logs], axis=0)
        suf = jnp.dot(stay, u_ref[...], preferred_element_type=F32)
        carry = carry_ref[...]
        ws = [None] * n_blk
        for i in reversed(range(n_blk)):
            ls, lstay = logs[i]
            ws[i] = jnp.exp2(ls + suf[i * rows:(i + 1) * rows] + carry).astype(BF16)
            carry = carry + jnp.sum(lstay, axis=1, keepdims=True)
        carry_ref[...] = carry
        return jnp.concatenate(ws, axis=1)

    @pl.when(step == 0)
    def _init():
        q = q_ref[...].astype(F32)
        q_rep = jnp.concatenate([q] * SB_HEADS, axis=0)
        r = lax.broadcasted_iota(jnp.int32, (rows, SB_WIDTH), 0) // t_new
        c = lax.broadcasted_iota(jnp.int32, (rows, SB_WIDTH), 1) // SB_HD
        qbd_ref[...] = jnp.where(r == c, q_rep, 0.0).astype(BF16)
        rh = lax.broadcasted_iota(jnp.int32, (rows, page), 0) // t_new
        bv = jnp.zeros((rows, page), F32)
        for h in range(SB_HEADS):
            bv = jnp.where(rh == h, bias_ref[h] * LOG2E, bv)
        biasv_ref[...] = bv
        acc_ref[...] = jnp.zeros_like(acc_ref)
        carry_ref[...] = jnp.zeros_like(carry_ref)
        pad = jnp.zeros((page - t_new, SB_WIDTH), F32)
        kn = jnp.concatenate([kn_ref[...].astype(F32), pad], axis=0).astype(BF16)
        vn = jnp.concatenate([vn_ref[...].astype(F32), pad], axis=0).astype(BF16)
        i = lax.broadcasted_iota(jnp.int32, (rows, page), 0) % t_new
        j = lax.broadcasted_iota(jnp.int32, (rows, page), 1)
        y = lax.dot_general(qbd_ref[...], kn, contract_last, preferred_element_type=F32)
        acc_ref[...] += jnp.dot(weights(y, j < i), vn, preferred_element_type=F32)

    k_t = jnp.concatenate([r[...] for r in k_refs], axis=1).astype(BF16)
    y = jnp.dot(qbd_ref[...], k_t, preferred_element_type=F32)
    w = weights(y, None)
    v_t = jnp.concatenate([r[...] for r in v_refs], axis=1).astype(BF16)
    acc_ref[...] += lax.dot_general(w, v_t, contract_last, preferred_element_type=F32)

    @pl.when(step == pl.num_programs(1) - 1)
    def _finish():
        lane = lax.broadcasted_iota(jnp.int32, (t_new, LANES), 1)
        for p in range(SB_WIDTH // LANES):
            cols = slice(p * LANES, (p + 1) * LANES)
            even = acc_ref[2 * p * t_new:(2 * p + 1) * t_new, cols]
            odd = acc_ref[(2 * p + 1) * t_new:(2 * p + 2) * t_new, cols]
            o_ref[:, cols] = jnp.where(lane < SB_HD, even, odd).astype(BF16)


def _sb_decode(q3, kn3, vn3, bias, cache_k, cache_v, page_table):
    b, t_new, _ = q3.shape
    n_pages = page_table.shape[1]
    n_phys, page_size = cache_k.shape[:2]
    assert page_size == LANES and t_new == SUBLANES
    g = DEC_PAGES_PER_STEP
    while n_pages % g:
        g //= 2
    n_steps = n_pages // g
    ck = jnp.transpose(cache_k, (0, 2, 3, 1)).reshape(n_phys, SB_WIDTH, page_size)
    cv = jnp.transpose(cache_v, (0, 2, 3, 1)).reshape(n_phys, SB_WIDTH, page_size)
    u = _suffix_matrix(page_size)
    rows = SB_HEADS * t_new
    new = pl.BlockSpec((None, t_new, SB_WIDTH), lambda bi, s, pt: (bi, 0, 0))

    def page_spec(i):
        return pl.BlockSpec((None, SB_WIDTH, page_size),
                            lambda bi, s, pt: (pt[bi, (n_steps - 1 - s) * g + i], 0, 0))

    grid_spec = pltpu.PrefetchScalarGridSpec(
        num_scalar_prefetch=1,
        grid=(b, n_steps),
        in_specs=[pl.BlockSpec(memory_space=pltpu.SMEM), new, new, new,
                  pl.BlockSpec(u.shape, lambda bi, s, pt: (0, 0))]
                 + [page_spec(i) for i in range(g)] * 2,
        out_specs=new,
        scratch_shapes=[pltpu.VMEM((rows, SB_WIDTH), BF16), pltpu.VMEM((rows, page_size), F32),
                        pltpu.VMEM((rows, SB_WIDTH), F32), pltpu.VMEM((rows, page_size), F32)],
    )
    return pl.pallas_call(
        functools.partial(_sb_decode_kernel, n_pages_step=g),
        grid_spec=grid_spec,
        out_shape=jax.ShapeDtypeStruct(q3.shape, BF16),
        compiler_params=_params("parallel", "arbitrary"),
        name="sb_decode",
    )(page_table, bias, q3, kn3, vn3, u, *([ck] * g), *([cv] * g))


def _ret_kernel(qr_ref, kr_ref, vr_ref, g_ref, s0_ref, dm_ref, qd_ref, kd_ref, cd_ref,
                o_ref, s_out_ref, s_scr, *, chunk):
    t = pl.program_id(1)
    tt = qr_ref.shape[0]
    pad = tt < chunk
    n_chunks = 1 if pad else tt // chunk

    @pl.when(t == 0)
    def _():
        s_scr[...] = s0_ref[...]

    def load(ref, rows, cols):
        x = ref[rows, cols]
        if pad:
            x = jnp.concatenate([x.astype(F32), jnp.zeros((chunk - tt, x.shape[1]), F32)], axis=0)
        return x

    pairs = [(h, c) for h in range(RET_HEADS) for c in range(n_chunks)]
    cols = lambda h: slice(h * RET_DK, (h + 1) * RET_DK)
    rows = lambda c: slice(0, tt) if pad else slice(c * chunk, (c + 1) * chunk)
    q = {hc: load(qr_ref, rows(hc[1]), cols(hc[0])).astype(BF16) for hc in pairs}
    k = {hc: load(kr_ref, rows(hc[1]), cols(hc[0])) for hc in pairs}
    v = {hc: load(vr_ref, rows(hc[1]), cols(hc[0])).astype(BF16) for hc in pairs}
    att = {hc: lax.dot_general(q[hc], k[hc].astype(BF16), (((1,), (1,)), ((), ())),
                               preferred_element_type=F32) * dm_ref[hc[0]] for hc in pairs}
    kv = {hc: lax.dot_general((k[hc].astype(F32) * kd_ref[hc[0]]).astype(BF16), v[hc],
                              (((0,), (0,)), ((), ())), preferred_element_type=F32)
          for hc in pairs}
    state = {}
    for h in range(RET_HEADS):
        s = s_scr[h]
        for c in range(n_chunks):
            state[h, c] = s
            s = s * cd_ref[h] + kv[h, c]
        s_scr[h] = s
    for h, c in pairs:
        o = jnp.dot(att[h, c].astype(BF16), v[h, c], preferred_element_type=F32)
        o = o + jnp.dot(q[h, c], state[h, c].astype(BF16), preferred_element_type=F32) * qd_ref[h]
        mu = jnp.mean(o, axis=-1, keepdims=True)
        d = o - mu
        var = jnp.mean(d * d, axis=-1, keepdims=True)
        on = d * lax.rsqrt(var + LN_EPS)
        gate = g_ref[rows(c), cols(h)]
        swish = gate * (1.0 / (1.0 + jnp.exp(-gate)))
        o_ref[rows(c), cols(h)] = ((on[:tt] if pad else on) * swish).astype(BF16)

    @pl.when(t == pl.num_programs(1) - 1)
    def _():
        s_out_ref[...] = s_scr[...]


def _retention_tables(chunk, true_chunk):
    lg = jnp.log(1.0 - 2.0 ** (-5.0 - jnp.arange(RET_HEADS, dtype=F32)))
    idx = jnp.arange(chunk, dtype=F32)
    diff = idx[:, None] - idx[None, :]
    dm = jnp.where(diff[None] >= 0, jnp.exp(jnp.maximum(diff, 0.0)[None] * lg[:, None, None]), 0.0)
    qd = jnp.exp((idx[None, :] + 1.0) * lg[:, None])
    kd = jnp.exp((true_chunk - 1.0 - idx)[None, :] * lg[:, None])
    kd = jnp.where(idx[None, :] < true_chunk, kd, 0.0)
    cd = jnp.exp(true_chunk * lg)
    rep = lambda a: jnp.broadcast_to(a[:, :, None], (RET_HEADS, chunk, RET_DK))
    cd = jnp.broadcast_to(cd[:, None, None], (RET_HEADS, SUBLANES, RET_DK))
    return dm, rep(qd), rep(kd), cd[:, :1, :]


def _retention(qr3, kr3, vr3, g3, s0):
    b, t, _ = qr3.shape
    chunk = RET_CHUNK
    true_chunk = chunk if t % chunk == 0 else t
    assert true_chunk <= chunk
    tt = min(ROW_TILE, t)
    dm, qd, kd, cd = _retention_tables(chunk, true_chunk)
    seq = pl.BlockSpec((None, tt, RET_WIDTH), lambda bi, ti: (bi, ti, 0))
    state = pl.BlockSpec((None,) + s0.shape[1:], lambda bi, ti: (bi, 0, 0, 0))
    return pl.pallas_call(
        functools.partial(_ret_kernel, chunk=chunk),
        grid=(b, t // tt),
        in_specs=[seq, seq, seq, seq, state] + [_resident(a.shape) for a in (dm, qd, kd, cd)],
        out_specs=[seq, state],
        out_shape=[jax.ShapeDtypeStruct(qr3.shape, BF16), jax.ShapeDtypeStruct(s0.shape, F32)],
        scratch_shapes=[pltpu.VMEM(s0.shape[1:], F32)],
        compiler_params=_params("parallel", "arbitrary"),
        name="retention",
    )(qr3, kr3, vr3, g3, s0, dm, qd, kd, cd)


def _gelu_tanh(x):
    return x * (0.5 * (1.0 + jnp.tanh(0.7978845608028654 * (x + 0.044715 * (x * x * x)))))


def _post_kernel(osb_ref, oret_ref, x_ref, prev_ref, wo_ref, g1_ref, b1_ref, wup_ref, cw_ref,
                 cb_ref, wdn_ref, g_ref, b_ref, y_ref, cs_ref, carry_scr, *, n_seq, use_carry):
    rows = x_ref.shape[0]
    tt = rows // n_seq
    if use_carry:
        @pl.when(pl.program_id(1) == 0)
        def _():
            carry_scr[...] = prev_ref[...]
        prev = carry_scr
    else:
        prev = prev_ref

    mix = jnp.dot(osb_ref[...], wo_ref[:SB_WIDTH, :], preferred_element_type=F32)
    mix = mix + jnp.dot(oret_ref[...], wo_ref[SB_WIDTH:, :], preferred_element_type=F32)
    h = _layer_norm(ALPHA * x_ref[...] + mix, g1_ref[...], b1_ref[...])
    hb = h.astype(BF16)
    cw = D_FF // FFN_CHUNKS
    tpos = lax.broadcasted_iota(jnp.int32, (n_seq, tt, cw), 1)

    def conv_cols(c0):
        u = jnp.dot(hb, wup_ref[:, c0:c0 + cw], preferred_element_type=F32)
        u3 = u.reshape(n_seq, tt, cw)
        p0 = prev[:, 0:1, c0:c0 + cw]
        p1 = prev[:, 1:2, c0:c0 + cw]
        r1 = pltpu.roll(u, 1, axis=0).reshape(n_seq, tt, cw)
        r2 = pltpu.roll(u, 2, axis=0).reshape(n_seq, tt, cw)
        um1 = jnp.where(tpos == 0, p1, r1)
        um2 = jnp.where(tpos == 0, p0, jnp.where(tpos == 1, p1, r2))
        w = cw_ref[:, c0:c0 + cw]
        c = cb_ref[:, c0:c0 + cw] + w[0:1] * um2 + w[1:2] * um1 + w[2:3] * u3
        last = u3[:, tt - 2:tt, :]
        cs_ref[:, :, c0:c0 + cw] = last
        if use_carry:
            carry_scr[:, :, c0:c0 + cw] = last
        return c.reshape(rows, cw)

    f = None
    for j in range(FFN_CHUNKS):
        ca = conv_cols(j * cw)
        cb = conv_cols(D_FF + j * cw)
        gated = (_gelu_tanh(ca) * cb).astype(BF16)
        part = jnp.dot(gated, wdn_ref[j * cw:(j + 1) * cw, :], preferred_element_type=F32)
        f = part if f is None else f + part
    y_ref[...] = _layer_norm(ALPHA * h + f, g_ref[...], b_ref[...])


def _post(o_sb3, o_ret3, x3, conv0, w):
    b, t, _ = x3.shape
    weights = [w[name] for name in ("w_o", "ln1_g", "ln1_b", "w_up", "conv_w", "conv_b", "w_down",
                                    "ln2_g", "ln2_b")]
    if t >= FFN_ROW_TILE:
        tm, n_seq = FFN_ROW_TILE, 1
        acts = (o_sb3, o_ret3, x3)
        grid = (b, t // tm)
        row_spec = lambda width: pl.BlockSpec((None, tm, width), lambda bi, ti: (bi, ti, 0))
        cspec = pl.BlockSpec((1, CONV_W - 1, 2 * D_FF), lambda bi, ti: (bi, 0, 0))
    else:
        n_seq = FFN_SHORT_ROWS // t
        while b % n_seq:
            n_seq //= 2
        tm = n_seq * t
        acts = tuple(a.reshape(b * t, a.shape[-1]) for a in (o_sb3, o_ret3, x3))
        grid = (b // n_seq, 1)
        row_spec = lambda width: pl.BlockSpec((tm, width), lambda bi, ti: (bi, 0))
        cspec = pl.BlockSpec((n_seq, CONV_W - 1, 2 * D_FF), lambda bi, ti: (bi, 0, 0))
    y, cs = pl.pallas_call(
        functools.partial(_post_kernel, n_seq=n_seq, use_carry=t >= FFN_ROW_TILE),
        grid=grid,
        in_specs=[row_spec(SB_WIDTH), row_spec(RET_WIDTH), row_spec(D_MODEL), cspec]
                 + [_resident(a.shape) for a in weights],
        out_specs=[row_spec(D_MODEL), cspec],
        out_shape=[jax.ShapeDtypeStruct(acts[2].shape, F32), jax.ShapeDtypeStruct(conv0.shape, F32)],
        scratch_shapes=[pltpu.VMEM((n_seq, CONV_W - 1, 2 * D_FF), F32)],
        compiler_params=_params("parallel", "arbitrary"),
        name="mix_conv_ffn",
    )(*acts, conv0, *weights)
    return y.reshape(x3.shape), cs


def _rotation_tables(pos):
    half = RET_DK // 2
    inv = 1.0 / (ROT_BASE ** jnp.linspace(0.0, 1.0, half, dtype=F32))
    ang = pos.astype(F32)[:, None] * inv[None, :]
    cos, sin = jnp.cos(ang), jnp.sin(ang)
    return jnp.concatenate([cos, cos], axis=-1), jnp.concatenate([-sin, sin], axis=-1)


def _hybrid_layer(x, pos0, attend, s_ret0, conv_buf, w):
    b, t, _ = x.shape
    n = b * t
    x2 = x.reshape(n, D_MODEL)
    cos_t, sin_t = _rotation_tables(pos0 + jnp.arange(t))
    tm = min(ROW_TILE, n)
    if t < tm:
        cos_t, sin_t = jnp.tile(cos_t, (tm // t, 1)), jnp.tile(sin_t, (tm // t, 1))
    q, k, v, kb, vb, qr, kr, vr, gate = _proj(x2, w["w_in"], cos_t, sin_t)
    three = lambda a: a.reshape(b, t, a.shape[-1])
    heads = lambda a: a.reshape(b, t, SB_HEADS, SB_HD)
    k, v = heads(k), heads(v)
    o_sb = attend(three(q), three(kb), three(vb), w["sb_bias"])
    o_ret, s_new = _retention(three(qr), three(kr), three(vr), three(gate), s_ret0)
    y, conv_new = _post(o_sb, o_ret, x, conv_buf, w)
    return y, k, v, s_new, conv_new


def kernel(x_prompt, x_sample, cache_k, cache_v, page_table, state_ret, state_conv, w_in, sb_bias,
           w_o, ln1_g, ln1_b, w_up, conv_w, conv_b, w_down, ln2_g, ln2_b):
    xp, xs = x_prompt, x_sample
    outs = [[] for _ in range(8)]
    for l in range(DEPTH):
        w = dict(w_in=w_in[l].astype(BF16), sb_bias=sb_bias[l], w_o=w_o[l].astype(BF16),
                 ln1_g=ln1_g[l][None], ln1_b=ln1_b[l][None], w_up=w_up[l].astype(BF16),
                 conv_w=conv_w[l], conv_b=conv_b[l][None], w_down=w_down[l].astype(BF16),
                 ln2_g=ln2_g[l][None], ln2_b=ln2_b[l][None])
        bp = xp.shape[0]
        s0 = jnp.zeros((bp, RET_HEADS, RET_DK, RET_DK), F32)
        c0 = jnp.zeros((bp, CONV_W - 1, 2 * D_FF), F32)
        xp, kp, vp, sp, cp = _hybrid_layer(xp, 0, _sb_prompt, s0, c0, w)
        past_len = page_table.shape[1] * cache_k.shape[2]
        attend = functools.partial(_sb_decode, cache_k=cache_k[l], cache_v=cache_v[l],
                                   page_table=page_table)
        xs, kk, vv, ss, cs = _hybrid_layer(xs, past_len, attend, state_ret[l], state_conv[l], w)
        for lst, a in zip(outs, (kp, vp, sp, cp, kk, vv, ss, cs)):
            lst.append(a)
    return (xp, xs) + tuple(jnp.stack(lst) for lst in outs)
```

```python
import functools

import jax
import jax.numpy as jnp
from jax import lax
from jax.experimental import pallas as pl
from jax.experimental.pallas import tpu as pltpu

F32 = jnp.float32
BF16 = jnp.bfloat16

D_MODEL = 1024
DEPTH = 1
SB_HD = 64
SB_WIDTH = D_MODEL // 2
SB_HEADS = SB_WIDTH // SB_HD
SB_SCALE = SB_HD ** -0.5
RET_HEADS = 4
RET_WIDTH = D_MODEL - SB_WIDTH
RET_DK = RET_WIDTH // RET_HEADS
RET_CHUNK = 128
ROT_BASE = 10000.0
PROJ_WIDTH = 3 * SB_WIDTH + 4 * RET_WIDTH
D_FF = ((8 * D_MODEL // 3 + 127) // 128) * 128
CONV_W = 3
ALPHA = (2.0 * DEPTH) ** 0.25
LN_EPS = 1e-5
LOG2E = 1.4426950408889634
MASK_BIAS = -1e30

LANES = 128
SUBLANES = 8
VMEM_LIMIT = 56 * 1024 * 1024

ROW_TILE = 512
FFN_ROW_TILE = 512
FFN_SHORT_ROWS = 256
FFN_CHUNKS = 2
SB_KEY_BLOCK = 256
SB_QUERY_TILE = 512
EDGE_ROWS = 16
DEC_PAGES_PER_STEP = 32


def _params(*sem):
    return pltpu.CompilerParams(dimension_semantics=sem, vmem_limit_bytes=VMEM_LIMIT)


def _resident(shape):
    nd = len(shape)
    return pl.BlockSpec(shape, lambda *_: (0,) * nd, pipeline_mode=pl.Buffered(1))


def _layer_norm(x, g, b):
    mu = jnp.mean(x, axis=-1, keepdims=True)
    d = x - mu
    var = jnp.mean(d * d, axis=-1, keepdims=True)
    return d * lax.rsqrt(var + LN_EPS) * g + b


def _proj_kernel(x_ref, w_ref, cos_ref, sin_ref,
                 q_o, k_o, v_o, kb_o, vb_o, qr_o, kr_o, vr_o, g_o):
    xb = x_ref[...].astype(BF16)

    def mm(c0, width):
        return jnp.dot(xb, w_ref[:, c0:c0 + width], preferred_element_type=F32)

    q_o[...] = (mm(0, SB_WIDTH) * (SB_SCALE * LOG2E)).astype(BF16)
    def store_heads(out_ref, a):
        for h in range(SB_HEADS):
            out_ref[pl.ds(h, a.shape[0], stride=SB_HEADS), :] = a[:, h * SB_HD:(h + 1) * SB_HD]

    k = mm(SB_WIDTH, SB_WIDTH)
    store_heads(k_o, k)
    kb_o[...] = k.astype(BF16)
    v = mm(2 * SB_WIDTH, SB_WIDTH)
    store_heads(v_o, v)
    vb_o[...] = v.astype(BF16)

    cos = cos_ref[...]
    sin = sin_ref[...]

    def rotate_into(out_ref, xr, scale):
        for h in range(RET_HEADS):
            xh = xr[:, h * RET_DK:(h + 1) * RET_DK]
            r = xh * cos + pltpu.roll(xh, RET_DK // 2, axis=1) * sin
            if scale is not None:
                r = r * scale
            out_ref[:, h * RET_DK:(h + 1) * RET_DK] = r.astype(BF16)

    base = 3 * SB_WIDTH
    rotate_into(qr_o, mm(base, RET_WIDTH), None)
    rotate_into(kr_o, mm(base + RET_WIDTH, RET_WIDTH), RET_DK ** -0.5)
    vr_o[...] = mm(base + 2 * RET_WIDTH, RET_WIDTH).astype(BF16)
    g_o[...] = mm(base + 3 * RET_WIDTH, RET_WIDTH)


def _proj(x2, w_in_bf, cos_t, sin_t):
    n = x2.shape[0]
    tm = min(ROW_TILE, n)
    n_tab = cos_t.shape[0] // tm
    row = lambda i: (i, 0)
    tab = lambda i: (i % n_tab, 0)
    half = lambda dt: jax.ShapeDtypeStruct((n, SB_WIDTH), dt)
    spec = pl.BlockSpec((tm, SB_WIDTH), row)
    by_head = jax.ShapeDtypeStruct((n * SB_HEADS, SB_HD), F32)
    by_head_spec = pl.BlockSpec((tm * SB_HEADS, SB_HD), row)
    return pl.pallas_call(
        _proj_kernel,
        grid=(n // tm,),
        in_specs=[pl.BlockSpec((tm, D_MODEL), row), _resident(w_in_bf.shape),
                  pl.BlockSpec((tm, RET_DK), tab), pl.BlockSpec((tm, RET_DK), tab)],
        out_specs=[spec, by_head_spec, by_head_spec] + [spec] * 6,
        out_shape=[half(BF16), by_head, by_head, half(BF16), half(BF16),
                   half(BF16), half(BF16), half(BF16), half(F32)],
        compiler_params=_params("parallel"),
        name="proj",
    )(x2, w_in_bf, cos_t, sin_t)


def _neg_abs(x):
    bits = lax.bitcast_convert_type(x, jnp.uint32) | jnp.uint32(0x80000000)
    return lax.bitcast_convert_type(bits, F32)


def _log2_sigmoids(y):
    ls = jnp.minimum(y, 0.0) - jnp.log2(1.0 + jnp.exp2(_neg_abs(y)))
    return ls, ls - y


def _suffix_matrix(n):
    j = lax.broadcasted_iota(jnp.int32, (n, n), 0)
    s = lax.broadcasted_iota(jnp.int32, (n, n), 1)
    return (j > s).astype(BF16)


def _sb_prompt_kernel(bias_ref, q_ref, k_ref, v_ref, u_ref, zero_ref, o_ref, mb_ref, acc_ref, carry_ref,
                      s0, s1, ls0, ls1, hi0, hi1, tot0, tot1, w0, w1, *, tq):
    pair = pl.program_id(1)
    tk = u_ref.shape[0]
    ratio = tq // tk
    nq = q_ref.shape[0] // tq
    n_items = ratio * nq * (nq + 1) // 2
    n_lanes = tk // LANES
    lane = lax.broadcasted_iota(jnp.int32, (tq, LANES), 1)
    low = lane < SB_HD
    s_buf, ls_buf, hi_buf, tot_buf, w_buf = (s0, s1), (ls0, ls1), (hi0, hi1), (tot0, tot1), (w0, w1)

    row = lax.broadcasted_iota(jnp.int32, (tq, tk), 0)
    col = lax.broadcasted_iota(jnp.int32, (tq, tk), 1)
    for h in range(2):
        b2 = bias_ref[2 * pair + h] * LOG2E
        for j in range(ratio):
            causal = col + (ratio - 1 - j) * tk < row
            mb_ref[h, j] = jnp.where(causal, 0.0, MASK_BIAS) + b2
        mb_ref[h, ratio] = jnp.zeros((tq, tk), F32) + b2
    for ref in (acc_ref, carry_ref) + s_buf + ls_buf + hi_buf + tot_buf + w_buf:
        ref[...] = jnp.zeros_like(ref)

    def key_block(qi, j):
        return pl.ds(pl.multiple_of(((qi + 1) * ratio - 1 - j) * tk, tk), tk)

    def step(t, items, p):
        (qi0, j0), (_, j1), (_, j2), (qi3, j3) = items
        wr, rd = p, 1 - p

        def scores(after):
            q = q_ref[pl.ds(pl.multiple_of(qi0 * tq, tq), tq), :]
            k = k_ref[key_block(qi0, j0), :]
            edge = lax.bitcast_convert_type(after, jnp.uint32) & zero_ref[...]
            head = lax.bitcast_convert_type(q[:EDGE_ROWS].astype(F32), jnp.uint32) | edge
            q = jnp.concatenate(
                [lax.bitcast_convert_type(head, F32).astype(BF16), q[EDGE_ROWS:]], axis=0)
            zero = jnp.zeros_like(q)
            for h, qh in enumerate((jnp.where(low, q, zero), jnp.where(low, zero, q))):
                s_buf[wr][h] = lax.dot_general(qh, k, (((1,), (1,)), ((), ())),
                                               preferred_element_type=F32)

        def logits():
            variant = jnp.minimum(j1, ratio)
            for h in range(2):
                ls, lstay = _log2_sigmoids(s_buf[rd][h] + mb_ref[h, variant])
                ls_buf[wr][h] = ls
                hi_buf[wr][h] = lstay.astype(BF16)
                tot = jnp.broadcast_to(jnp.sum(lstay, axis=1, keepdims=True), (tq, LANES))
                tot_buf[wr][h] = tot
            return tot[tq - EDGE_ROWS:]

        def weights():
            for h in range(2):
                suf = jnp.dot(hi_buf[rd][h], u_ref[...], preferred_element_type=F32)
                c = carry_ref[h]
                cc = jnp.concatenate([c] * n_lanes, axis=1)
                w_buf[wr][h] = jnp.exp2(ls_buf[rd][h] + suf + cc).astype(BF16)
                carry_ref[h] = c + tot_buf[rd][h]

        def outputs():
            v = v_ref[key_block(qi3, j3), :]
            zero = jnp.zeros_like(v)
            low_k = lax.broadcasted_iota(jnp.int32, v.shape, 1) < SB_HD
            v2 = jnp.concatenate([jnp.where(low_k, v, zero), jnp.where(low_k, zero, v)], axis=0)
            w2 = jnp.concatenate([w_buf[rd][0], w_buf[rd][1]], axis=1)
            acc_ref[...] += jnp.dot(w2, v2, preferred_element_type=F32)

        outputs()
        scores(logits())
        weights()

        @pl.when((t >= 3) & (t - 3 < n_items) & (j3 == (qi3 + 1) * ratio - 1))
        def _():
            o_ref[pl.ds(pl.multiple_of(qi3 * tq, tq), tq), :] = (
                acc_ref[...].astype(BF16))

        @pl.when(j2 == 0)
        def _():
            acc_ref[...] = jnp.zeros_like(acc_ref)

        @pl.when(j1 == 0)
        def _():
            carry_ref[...] = jnp.zeros_like(carry_ref)

        wrap = j0 == (qi0 + 1) * ratio - 1
        done = wrap & (qi0 == nq - 1)
        nxt = (jnp.where(wrap & ~done, qi0 + 1, qi0), jnp.where(done, j0, jnp.where(wrap, 0, j0 + 1)))
        return (nxt,) + items[:3]

    def body(i, items):
        return step(2 * i + 1, step(2 * i, items, 0), 1)

    zero_item = (jnp.int32(0), jnp.int32(0))
    lax.fori_loop(0, (n_items + 3 + 1) // 2, body, (zero_item,) * 4)


def _sb_prompt(q3, k3, v3, bias):
    b, t, _ = q3.shape
    tk = SB_KEY_BLOCK
    tq = SB_QUERY_TILE if t % SB_QUERY_TILE == 0 else tk
    n_pairs = SB_WIDTH // LANES
    u = _suffix_matrix(tk)
    zero_bits = jnp.zeros((EDGE_ROWS, LANES), jnp.uint32)
    spec = pl.BlockSpec((None, t, LANES), lambda bi, p: (bi, 0, p))
    f32 = lambda n: pltpu.VMEM((2, tq, n), F32)
    bf16 = lambda n: pltpu.VMEM((2, tq, n), BF16)
    return pl.pallas_call(
        functools.partial(_sb_prompt_kernel, tq=tq),
        grid=(b, n_pairs),
        in_specs=[pl.BlockSpec(memory_space=pltpu.SMEM), spec, spec, spec, _resident(u.shape),
                  _resident(zero_bits.shape)],
        out_specs=spec,
        out_shape=jax.ShapeDtypeStruct(q3.shape, BF16),
        scratch_shapes=[pltpu.VMEM((2, tq // tk + 1, tq, tk), F32), pltpu.VMEM((tq, LANES), F32),
                        f32(LANES),
                        f32(tk), f32(tk), f32(tk), f32(tk), bf16(tk), bf16(tk),
                        f32(LANES), f32(LANES), bf16(tk), bf16(tk)],
        compiler_params=_params("parallel", "parallel"),
        name="sb_prompt",
    )(bias, q3, k3, v3, u, zero_bits)


def _sb_decode_kernel(pt_ref, bias_ref, q_ref, kn_ref, vn_ref, u_ref, *rest, n_pages_step):
    del pt_ref
    g = n_pages_step
    k_refs, v_refs = rest[:g], rest[g:2 * g]
    o_ref, qbd_ref, biasv_ref, acc_ref, carry_ref = rest[2 * g:]
    step = pl.program_id(1)
    t_new = q_ref.shape[0]
    rows, page = biasv_ref.shape
    contract_last = (((1,), (1,)), ((), ()))

    def weights(y, mask):
        n_blk = y.shape[1] // page
        logs = []
        for i in range(n_blk):
            yi = y[:, i * page:(i + 1) * page] + biasv_ref[...]
            if mask is not None:
                yi = jnp.where(mask, yi, MASK_BIAS)
            logs.append(_log2_sigmoids(yi))
        stay = jnp.concatenate([lstay.astype(BF16) for _, lstay in logs], axis=0)
        suf = jnp.dot(stay, u_ref[...], preferred_element_type=F32)
        carry = carry_ref[...]
        ws = [None] * n_blk
        for i in reversed(range(n_blk)):
            ls, lstay = logs[i]
            ws[i] = jnp.exp2(ls + suf[i * rows:(i + 1) * rows] + carry).astype(BF16)
            carry = carry + jnp.sum(lstay, axis=1, keepdims=True)
        carry_ref[...] = carry
        return jnp.concatenate(ws, axis=1)

    @pl.when(step == 0)
    def _init():
        q = q_ref[...].astype(F32)
        q_rep = jnp.concatenate([q] * SB_HEADS, axis=0)
        r = lax.broadcasted_iota(jnp.int32, (rows, SB_WIDTH), 0) // t_new
        c = lax.broadcasted_iota(jnp.int32, (rows, SB_WIDTH), 1) // SB_HD
        qbd_ref[...] = jnp.where(r == c, q_rep, 0.0).astype(BF16)
        rh = lax.broadcasted_iota(jnp.int32, (rows, page), 0) // t_new
        bv = jnp.zeros((rows, page), F32)
        for h in range(SB_HEADS):
            bv = jnp.where(rh == h, bias_ref[h] * LOG2E, bv)
        biasv_ref[...] = bv
        acc_ref[...] = jnp.zeros_like(acc_ref)
        carry_ref[...] = jnp.zeros_like(carry_ref)
        pad = jnp.zeros((page - t_new, SB_WIDTH), F32)
        kn = jnp.concatenate([kn_ref[...].astype(F32), pad], axis=0).astype(BF16)
        vn = jnp.concatenate([vn_ref[...].astype(F32), pad], axis=0).astype(BF16)
        i = lax.broadcasted_iota(jnp.int32, (rows, page), 0) % t_new
        j = lax.broadcasted_iota(jnp.int32, (rows, page), 1)
        y = lax.dot_general(qbd_ref[...], kn, contract_last, preferred_element_type=F32)
        acc_ref[...] += jnp.dot(weights(y, j < i), vn, preferred_element_type=F32)

    k_t = jnp.concatenate([r[...] for r in k_refs], axis=1).astype(BF16)
    y = jnp.dot(qbd_ref[...], k_t, preferred_element_type=F32)
    w = weights(y, None)
    v_t = jnp.concatenate([r[...] for r in v_refs], axis=1).astype(BF16)
    acc_ref[...] += lax.dot_general(w, v_t, contract_last, preferred_element_type=F32)

    @pl.when(step == pl.num_programs(1) - 1)
    def _finish():
        lane = lax.broadcasted_iota(jnp.int32, (t_new, LANES), 1)
        for p in range(SB_WIDTH // LANES):
            cols = slice(p * LANES, (p + 1) * LANES)
            even = acc_ref[2 * p * t_new:(2 * p + 1) * t_new, cols]
            odd = acc_ref[(2 * p + 1) * t_new:(2 * p + 2) * t_new, cols]
            o_ref[:, cols] = jnp.where(lane < SB_HD, even, odd).astype(BF16)


def _sb_decode(q3, kn3, vn3, bias, cache_k, cache_v, page_table):
    b, t_new, _ = q3.shape
    n_pages = page_table.shape[1]
    n_phys, page_size = cache_k.shape[:2]
    assert page_size == LANES and t_new == SUBLANES
    g = DEC_PAGES_PER_STEP
    while n_pages % g:
        g //= 2
    n_steps = n_pages // g
    ck = jnp.transpose(cache_k, (0, 2, 3, 1)).reshape(n_phys, SB_WIDTH, page_size)
    cv = jnp.transpose(cache_v, (0, 2, 3, 1)).reshape(n_phys, SB_WIDTH, page_size)
    u = _suffix_matrix(page_size)
    rows = SB_HEADS * t_new
    new = pl.BlockSpec((None, t_new, SB_WIDTH), lambda bi, s, pt: (bi, 0, 0))

    def page_spec(i):
        return pl.BlockSpec((None, SB_WIDTH, page_size),
                            lambda bi, s, pt: (pt[bi, (n_steps - 1 - s) * g + i], 0, 0))

    grid_spec = pltpu.PrefetchScalarGridSpec(
        num_scalar_prefetch=1,
        grid=(b, n_steps),
        in_specs=[pl.BlockSpec(memory_space=pltpu.SMEM), new, new, new,
                  pl.BlockSpec(u.shape, lambda bi, s, pt: (0, 0))]
                 + [page_spec(i) for i in range(g)] * 2,
        out_specs=new,
        scratch_shapes=[pltpu.VMEM((rows, SB_WIDTH), BF16), pltpu.VMEM((rows, page_size), F32),
                        pltpu.VMEM((rows, SB_WIDTH), F32), pltpu.VMEM((rows, page_size), F32)],
    )
    return pl.pallas_call(
        functools.partial(_sb_decode_kernel, n_pages_step=g),
        grid_spec=grid_spec,
        out_shape=jax.ShapeDtypeStruct(q3.shape, BF16),
        compiler_params=_params("parallel", "arbitrary"),
        name="sb_decode",
    )(page_table, bias, q3, kn3, vn3, u, *([ck] * g), *([cv] * g))


def _ret_kernel(qr_ref, kr_ref, vr_ref, g_ref, s0_ref, dm_ref, qd_ref, kd_ref, cd_ref,
                o_ref, s_out_ref, s_scr, *, chunk):
    t = pl.program_id(1)
    tt = qr_ref.shape[0]
    pad = tt < chunk
    n_chunks = 1 if pad else tt // chunk

    @pl.when(t == 0)
    def _():
        s_scr[...] = s0_ref[...]

    def load(ref, rows, cols):
        x = ref[rows, cols]
        if pad:
            x = jnp.concatenate([x.astype(F32), jnp.zeros((chunk - tt, x.shape[1]), F32)], axis=0)
        return x

    pairs = [(h, c) for h in range(RET_HEADS) for c in range(n_chunks)]
    cols = lambda h: slice(h * RET_DK, (h + 1) * RET_DK)
    rows = lambda c: slice(0, tt) if pad else slice(c * chunk, (c + 1) * chunk)
    q = {hc: load(qr_ref, rows(hc[1]), cols(hc[0])).astype(BF16) for hc in pairs}
    k = {hc: load(kr_ref, rows(hc[1]), cols(hc[0])) for hc in pairs}
    v = {hc: load(vr_ref, rows(hc[1]), cols(hc[0])).astype(BF16) for hc in pairs}
    att = {hc: lax.dot_general(q[hc], k[hc].astype(BF16), (((1,), (1,)), ((), ())),
                               preferred_element_type=F32) * dm_ref[hc[0]] for hc in pairs}
    kv = {hc: lax.dot_general((k[hc].astype(F32) * kd_ref[hc[0]]).astype(BF16), v[hc],
                              (((0,), (0,)), ((), ())), preferred_element_type=F32)
          for hc in pairs}
    state = {}
    for h in range(RET_HEADS):
        s = s_scr[h]
        for c in range(n_chunks):
            state[h, c] = s
            s = s * cd_ref[h] + kv[h, c]
        s_scr[h] = s
    for h, c in pairs:
        o = jnp.dot(att[h, c].astype(BF16), v[h, c], preferred_element_type=F32)
        o = o + jnp.dot(q[h, c], state[h, c].astype(BF16), preferred_element_type=F32) * qd_ref[h]
        mu = jnp.mean(o, axis=-1, keepdims=True)
        d = o - mu
        var = jnp.mean(d * d, axis=-1, keepdims=True)
        on = d * lax.rsqrt(var + LN_EPS)
        gate = g_ref[rows(c), cols(h)]
        swish = gate * (1.0 / (1.0 + jnp.exp(-gate)))
        o_ref[rows(c), cols(h)] = ((on[:tt] if pad else on) * swish).astype(BF16)

    @pl.when(t == pl.num_programs(1) - 1)
    def _():
        s_out_ref[...] = s_scr[...]


def _retention_tables(chunk, true_chunk):
    lg = jnp.log(1.0 - 2.0 ** (-5.0 - jnp.arange(RET_HEADS, dtype=F32)))
    idx = jnp.arange(chunk, dtype=F32)
    diff = idx[:, None] - idx[None, :]
    dm = jnp.where(diff[None] >= 0, jnp.exp(jnp.maximum(diff, 0.0)[None] * lg[:, None, None]), 0.0)
    qd = jnp.exp((idx[None, :] + 1.0) * lg[:, None])
    kd = jnp.exp((true_chunk - 1.0 - idx)[None, :] * lg[:, None])
    kd = jnp.where(idx[None, :] < true_chunk, kd, 0.0)
    cd = jnp.exp(true_chunk * lg)
    rep = lambda a: jnp.broadcast_to(a[:, :, None], (RET_HEADS, chunk, RET_DK))
    cd = jnp.broadcast_to(cd[:, None, None], (RET_HEADS, SUBLANES, RET_DK))
    return dm, rep(qd), rep(kd), cd[:, :1, :]


def _retention(qr3, kr3, vr3, g3, s0):
    b, t, _ = qr3.shape
    chunk = RET_CHUNK
    true_chunk = chunk if t % chunk == 0 else t
    assert true_chunk <= chunk
    tt = min(ROW_TILE, t)
    dm, qd, kd, cd = _retention_tables(chunk, true_chunk)
    seq = pl.BlockSpec((None, tt, RET_WIDTH), lambda bi, ti: (bi, ti, 0))
    state = pl.BlockSpec((None,) + s0.shape[1:], lambda bi, ti: (bi, 0, 0, 0))
    return pl.pallas_call(
        functools.partial(_ret_kernel, chunk=chunk),
        grid=(b, t // tt),
        in_specs=[seq, seq, seq, seq, state] + [_resident(a.shape) for a in (dm, qd, kd, cd)],
        out_specs=[seq, state],
        out_shape=[jax.ShapeDtypeStruct(qr3.shape, BF16), jax.ShapeDtypeStruct(s0.shape, F32)],
        scratch_shapes=[pltpu.VMEM(s0.shape[1:], F32)],
        compiler_params=_params("parallel", "arbitrary"),
        name="retention",
    )(qr3, kr3, vr3, g3, s0, dm, qd, kd, cd)


def _gelu_tanh(x):
    return x * (0.5 * (1.0 + jnp.tanh(0.7978845608028654 * (x + 0.044715 * (x * x * x)))))


def _post_kernel(osb_ref, oret_ref, x_ref, prev_ref, wo_ref, g1_ref, b1_ref, wup_ref, cw_ref,
                 cb_ref, wdn_ref, g_ref, b_ref, y_ref, cs_ref, carry_scr, *, n_seq, use_carry):
    rows = x_ref.shape[0]
    tt = rows // n_seq
    if use_carry:
        @pl.when(pl.program_id(1) == 0)
        def _():
            carry_scr[...] = prev_ref[...]
        prev = carry_scr
    else:
        prev = prev_ref

    mix = jnp.dot(osb_ref[...], wo_ref[:SB_WIDTH, :], preferred_element_type=F32)
    mix = mix + jnp.dot(oret_ref[...], wo_ref[SB_WIDTH:, :], preferred_element_type=F32)
    h = _layer_norm(ALPHA * x_ref[...] + mix, g1_ref[...], b1_ref[...])
    hb = h.astype(BF16)
    cw = D_FF // FFN_CHUNKS
    tpos = lax.broadcasted_iota(jnp.int32, (n_seq, tt, cw), 1)

    def conv_cols(c0):
        u = jnp.dot(hb, wup_ref[:, c0:c0 + cw], preferred_element_type=F32)
        u3 = u.reshape(n_seq, tt, cw)
        p0 = prev[:, 0:1, c0:c0 + cw]
        p1 = prev[:, 1:2, c0:c0 + cw]
        r1 = pltpu.roll(u, 1, axis=0).reshape(n_seq, tt, cw)
        r2 = pltpu.roll(u, 2, axis=0).reshape(n_seq, tt, cw)
        um1 = jnp.where(tpos == 0, p1, r1)
        um2 = jnp.where(tpos == 0, p0, jnp.where(tpos == 1, p1, r2))
        w = cw_ref[:, c0:c0 + cw]
        c = cb_ref[:, c0:c0 + cw] + w[0:1] * um2 + w[1:2] * um1 + w[2:3] * u3
        last = u3[:, tt - 2:tt, :]
        cs_ref[:, :, c0:c0 + cw] = last
        if use_carry:
            carry_scr[:, :, c0:c0 + cw] = last
        return c.reshape(rows, cw)

    f = None
    for j in range(FFN_CHUNKS):
        ca = conv_cols(j * cw)
        cb = conv_cols(D_FF + j * cw)
        gated = (_gelu_tanh(ca) * cb).astype(BF16)
        part = jnp.dot(gated, wdn_ref[j * cw:(j + 1) * cw, :], preferred_element_type=F32)
        f = part if f is None else f + part
    y_ref[...] = _layer_norm(ALPHA * h + f, g_ref[...], b_ref[...])


def _post(o_sb3, o_ret3, x3, conv0, w):
    b, t, _ = x3.shape
    weights = [w[name] for name in ("w_o", "ln1_g", "ln1_b", "w_up", "conv_w", "conv_b", "w_down",
                                    "ln2_g", "ln2_b")]
    if t >= FFN_ROW_TILE:
        tm, n_seq = FFN_ROW_TILE, 1
        acts = (o_sb3, o_ret3, x3)
        grid = (b, t // tm)
        row_spec = lambda width: pl.BlockSpec((None, tm, width), lambda bi, ti: (bi, ti, 0))
        cspec = pl.BlockSpec((1, CONV_W - 1, 2 * D_FF), lambda bi, ti: (bi, 0, 0))
    else:
        n_seq = FFN_SHORT_ROWS // t
        while b % n_seq:
            n_seq //= 2
        tm = n_seq * t
        acts = tuple(a.reshape(b * t, a.shape[-1]) for a in (o_sb3, o_ret3, x3))
        grid = (b // n_seq, 1)
        row_spec = lambda width: pl.BlockSpec((tm, width), lambda bi, ti: (bi, 0))
        cspec = pl.BlockSpec((n_seq, CONV_W - 1, 2 * D_FF), lambda bi, ti: (bi, 0, 0))
    y, cs = pl.pallas_call(
        functools.partial(_post_kernel, n_seq=n_seq, use_carry=t >= FFN_ROW_TILE),
        grid=grid,
        in_specs=[row_spec(SB_WIDTH), row_spec(RET_WIDTH), row_spec(D_MODEL), cspec]
                 + [_resident(a.shape) for a in weights],
        out_specs=[row_spec(D_MODEL), cspec],
        out_shape=[jax.ShapeDtypeStruct(acts[2].shape, F32), jax.ShapeDtypeStruct(conv0.shape, F32)],
        scratch_shapes=[pltpu.VMEM((n_seq, CONV_W - 1, 2 * D_FF), F32)],
        compiler_params=_params("parallel", "arbitrary"),
        name="mix_conv_ffn",
    )(*acts, conv0, *weights)
    return y.reshape(x3.shape), cs


def _rotation_tables(pos):
    half = RET_DK // 2
    inv = 1.0 / (ROT_BASE ** jnp.linspace(0.0, 1.0, half, dtype=F32))
    ang = pos.astype(F32)[:, None] * inv[None, :]
    cos, sin = jnp.cos(ang), jnp.sin(ang)
    return jnp.concatenate([cos, cos], axis=-1), jnp.concatenate([-sin, sin], axis=-1)


def _hybrid_layer(x, pos0, attend, s_ret0, conv_buf, w):
    b, t, _ = x.shape
    n = b * t
    x2 = x.reshape(n, D_MODEL)
    cos_t, sin_t = _rotation_tables(pos0 + jnp.arange(t))
    tm = min(ROW_TILE, n)
    if t < tm:
        cos_t, sin_t = jnp.tile(cos_t, (tm // t, 1)), jnp.tile(sin_t, (tm // t, 1))
    q, k, v, kb, vb, qr, kr, vr, gate = _proj(x2, w["w_in"], cos_t, sin_t)
    three = lambda a: a.reshape(b, t, a.shape[-1])
    heads = lambda a: a.reshape(b, t, SB_HEADS, SB_HD)
    k, v = heads(k), heads(v)
    o_sb = attend(three(q), three(kb), three(vb), w["sb_bias"])
    o_ret, s_new = _retention(three(qr), three(kr), three(vr), three(gate), s_ret0)
    y, conv_new = _post(o_sb, o_ret, x, conv_buf, w)
    return y, k, v, s_new, conv_new


def kernel(x_prompt, x_sample, cache_k, cache_v, page_table, state_ret, state_conv, w_in, sb_bias,
           w_o, ln1_g, ln1_b, w_up, conv_w, conv_b, w_down, ln2_g, ln2_b):
    xp, xs = x_prompt, x_sample
    outs = [[] for _ in range(8)]
    for l in range(DEPTH):
        w = dict(w_in=w_in[l].astype(BF16), sb_bias=sb_bias[l], w_o=w_o[l].astype(BF16),
                 ln1_g=ln1_g[l][None], ln1_b=ln1_b[l][None], w_up=w_up[l].astype(BF16),
                 conv_w=conv_w[l], conv_b=conv_b[l][None], w_down=w_down[l].astype(BF16),
                 ln2_g=ln2_g[l][None], ln2_b=ln2_b[l][None])
        bp = xp.shape[0]
        s0 = jnp.zeros((bp, RET_HEADS, RET_DK, RET_DK), F32)
        c0 = jnp.zeros((bp, CONV_W - 1, 2 * D_FF), F32)
        xp, kp, vp, sp, cp = _hybrid_layer(xp, 0, _sb_prompt, s0, c0, w)
        past_len = page_table.shape[1] * cache_k.shape[2]
        attend = functools.partial(_sb_decode, cache_k=cache_k[l], cache_v=cache_v[l],
                                   page_table=page_table)
        xs, kk, vv, ss, cs = _hybrid_layer(xs, past_len, attend, state_ret[l], state_conv[l], w)
        for lst, a in zip(outs, (kp, vp, sp, cp, kk, vv, ss, cs)):
            lst.append(a)
    return (xp, xs) + tuple(jnp.stack(lst) for lst in outs)
```
